```python
import math
import jax, jax.numpy as jnp
from jax import lax
import numpy as np

D_MODEL = 4096
BATCH = 4
SEQ = 4096
DEPTH = 1

D_MIX = D_MODEL
D_GLA = D_MIX // 2
D_S5 = D_MIX - D_GLA
GLA_HEADS = 4
GLA_DK_TOTAL = D_GLA // 2
GLA_DK = GLA_DK_TOTAL // GLA_HEADS
GLA_DV = D_GLA // GLA_HEADS
GLA_GATE_RANK = 16
GLA_GATE_TAU = 16.0
GLA_CHUNK = 64
S5_GROUP = 16
S5_GROUPS = D_S5 // S5_GROUP
S5_STATE = 64
S5_DT_MIN = 1e-3
S5_DT_MAX = 1e-1
DEEPNORM_ALPHA = (2.0 * DEPTH) ** 0.25
DEEPNORM_BETA = (8.0 * DEPTH) ** -0.25
NORM_EPS = 1e-5
SPLIT_IDX = (
    GLA_DK_TOTAL,
    2 * GLA_DK_TOTAL,
    2 * GLA_DK_TOTAL + D_GLA,
    2 * GLA_DK_TOTAL + D_GLA + GLA_GATE_RANK,
    2 * GLA_DK_TOTAL + 2 * D_GLA + GLA_GATE_RANK,
    2 * GLA_DK_TOTAL + 2 * D_GLA + GLA_GATE_RANK + D_S5,
)
D_IN = 2 * GLA_DK_TOTAL + 2 * D_GLA + GLA_GATE_RANK + 2 * D_S5

kernel_name = 'hymba_gla_s5_deepnorm_adaln'


def layer_norm(r, g, b):
    r32 = r.astype(jnp.float32)
    mu = jnp.mean(r32, axis=-1, keepdims=True)
    var = jnp.mean(jnp.square(r32 - mu), axis=-1, keepdims=True)
    return ((r32 - mu) * lax.rsqrt(var + NORM_EPS) * g.astype(jnp.float32) + b.astype(jnp.float32)).astype(r.dtype)


def gla_chunked(q, k, v, log_a):
    bsz, seq, heads, dk = q.shape
    dv = v.shape[-1]
    n_chunks = seq // GLA_CHUNK

    def to_chunks(t):
        return t.reshape(bsz, n_chunks, GLA_CHUNK, heads, t.shape[-1]).transpose(1, 0, 3, 2, 4)

    qc, kc, vc, gc = to_chunks(q), to_chunks(k), to_chunks(v), to_chunks(log_a)
    causal = jnp.tril(jnp.ones((GLA_CHUNK, GLA_CHUNK), dtype=bool))

    def step(state, inp):
        qn, kn, vn, gn = inp
        b = jnp.cumsum(gn, axis=2)
        b_last = b[:, :, -1:, :]
        q_dec = qn * jnp.exp(b)
        k_inv = kn * jnp.exp(-b)
        att = jnp.where(causal, jnp.einsum('bhid,bhjd->bhij', q_dec, k_inv), 0.0)
        out = jnp.einsum('bhij,bhje->bhie', att, vn) + jnp.einsum('bhid,bhde->bhie', q_dec, state)
        k_end = kn * jnp.exp(b_last - b)
        state = state * jnp.exp(b_last[:, :, 0, :])[..., None] + jnp.einsum('bhcd,bhce->bhde', k_end, vn)
        return state, out

    s0 = jnp.zeros((bsz, heads, dk, dv), jnp.float32)
    _, o = lax.scan(step, s0, (qc, kc, vc, gc))
    return o.transpose(1, 0, 3, 2, 4).reshape(bsz, seq, heads, dv)


def s5_ssm(u, lam_re, lam_im, log_dt, b_re, b_im, c_re, c_im, d_skip):
    bsz, seq, _ = u.shape
    ug = u.reshape(bsz, seq, S5_GROUPS, S5_GROUP)
    dt = jnp.exp(log_dt)[:, None]
    z_re, z_im = lam_re * dt, lam_im * dt
    mag = jnp.exp(z_re)
    ab_re, ab_im = mag * jnp.cos(z_im), mag * jnp.sin(z_im)
    den = lam_re * lam_re + lam_im * lam_im
    n_re, n_im = ab_re - 1.0, ab_im
    f_re = ((n_re * lam_re + n_im * lam_im) / den)[..., None]
    f_im = ((n_im * lam_re - n_re * lam_im) / den)[..., None]
    bb_re = f_re * b_re - f_im * b_im
    bb_im = f_re * b_im + f_im * b_re
    bu_re = jnp.einsum('gph,blgh->blgp', bb_re, ug)
    bu_im = jnp.einsum('gph,blgh->blgp', bb_im, ug)
    a_re = jnp.broadcast_to(ab_re, bu_re.shape)
    a_im = jnp.broadcast_to(ab_im, bu_im.shape)

    def combine(e1, e2):
        a1r, a1i, b1r, b1i = e1
        a2r, a2i, b2r, b2i = e2
        return (a2r * a1r - a2i * a1i,
                a2r * a1i + a2i * a1r,
                a2r * b1r - a2i * b1i + b2r,
                a2r * b1i + a2i * b1r + b2i)

    _, _, s_re, s_im = lax.associative_scan(combine, (a_re, a_im, bu_re, bu_im), axis=1)
    y = jnp.einsum('ghp,blgp->blgh', c_re, s_re) - jnp.einsum('ghp,blgp->blgh', c_im, s_im)
    return y.reshape(bsz, seq, D_S5) + d_skip * u


def setup_inputs(seed: int = 0) -> dict:
    key = jax.random.key(seed)
    ks = jax.random.split(key, 24)
    f32 = jnp.float32
    nrm = lambda k, shape: jax.random.normal(k, shape, f32)
    x = nrm(ks[0], (BATCH, SEQ, D_MODEL))
    c = nrm(ks[1], (BATCH, D_MODEL))
    w_ada = nrm(ks[2], (DEPTH, D_MODEL, 3 * D_MODEL)) * D_MODEL ** -0.5
    b_ada = nrm(ks[3], (DEPTH, 3 * D_MODEL)) * 0.02
    w_in = nrm(ks[4], (DEPTH, D_MODEL, D_IN)) * D_MODEL ** -0.5
    w_gla_gate = nrm(ks[5], (DEPTH, GLA_GATE_RANK, GLA_DK_TOTAL)) * GLA_GATE_RANK ** -0.5
    b_gla_gate = nrm(ks[6], (DEPTH, GLA_DK_TOTAL)) * 0.1
    gla_norm_g = 1.0 + 0.02 * nrm(ks[7], (DEPTH, GLA_DV))
    n_idx = jnp.arange(S5_STATE, dtype=f32)
    s5_lambda_re = -0.5 * (1.0 + 0.01 * nrm(ks[8], (DEPTH, S5_GROUPS, S5_STATE)))
    s5_lambda_im = math.pi * n_idx + 0.01 * nrm(ks[9], (DEPTH, S5_GROUPS, S5_STATE))
    s5_log_dt = jax.random.uniform(ks[10], (DEPTH, S5_GROUPS), f32, math.log(S5_DT_MIN), math.log(S5_DT_MAX))
    s5_b_re = nrm(ks[11], (DEPTH, S5_GROUPS, S5_STATE, S5_GROUP)) * (2.0 * S5_GROUP) ** -0.5
    s5_b_im = nrm(ks[12], (DEPTH, S5_GROUPS, S5_STATE, S5_GROUP)) * (2.0 * S5_GROUP) ** -0.5
    s5_c_re = nrm(ks[13], (DEPTH, S5_GROUPS, S5_GROUP, S5_STATE)) * (2.0 * S5_STATE) ** -0.5
    s5_c_im = nrm(ks[14], (DEPTH, S5_GROUPS, S5_GROUP, S5_STATE)) * (2.0 * S5_STATE) ** -0.5
    s5_d = nrm(ks[15], (DEPTH, D_S5))
    w_glu = nrm(ks[16], (DEPTH, D_S5, D_S5)) * D_S5 ** -0.5
    b_glu = nrm(ks[17], (DEPTH, D_S5)) * 0.02
    w_out = nrm(ks[18], (DEPTH, D_MIX, D_MODEL)) * (D_MIX ** -0.5) * DEEPNORM_BETA
    ln_g = 1.0 + 0.02 * nrm(ks[19], (DEPTH, D_MODEL))
    ln_b = 0.02 * nrm(ks[20], (DEPTH, D_MODEL))
    return {'x': x, 'c': c, 'w_ada': w_ada, 'b_ada': b_ada, 'w_in': w_in,
            'w_gla_gate': w_gla_gate, 'b_gla_gate': b_gla_gate, 'gla_norm_g': gla_norm_g,
            's5_lambda_re': s5_lambda_re, 's5_lambda_im': s5_lambda_im, 's5_log_dt': s5_log_dt,
            's5_b_re': s5_b_re, 's5_b_im': s5_b_im, 's5_c_re': s5_c_re, 's5_c_im': s5_c_im,
            's5_d': s5_d, 'w_glu': w_glu, 'b_glu': b_glu, 'w_out': w_out,
            'ln_g': ln_g, 'ln_b': ln_b}


def reference(x, c, w_ada, b_ada, w_in, w_gla_gate, b_gla_gate, gla_norm_g,
              s5_lambda_re, s5_lambda_im, s5_log_dt, s5_b_re, s5_b_im, s5_c_re, s5_c_im,
              s5_d, w_glu, b_glu, w_out, ln_g, ln_b):
    f32 = jnp.float32
    bsz, seq, _ = x.shape
    for layer in range(DEPTH):
        mod = jax.nn.silu(c) @ w_ada[layer] + b_ada[layer]
        shift, scale, gate = jnp.split(mod[:, None, :], 3, axis=-1)
        h = x * (1.0 + scale) + shift
        proj = h @ w_in[layer]
        q, k, v, g_lr, z_gla, u_s5, z_s5 = jnp.split(proj, SPLIT_IDX, axis=-1)

        qh = q.astype(f32).reshape(bsz, seq, GLA_HEADS, GLA_DK) * GLA_DK ** -0.5
        kh = k.astype(f32).reshape(bsz, seq, GLA_HEADS, GLA_DK)
        vh = v.astype(f32).reshape(bsz, seq, GLA_HEADS, GLA_DV)
        gate_logit = (g_lr @ w_gla_gate[layer] + b_gla_gate[layer]).astype(f32)
        log_a = (jax.nn.log_sigmoid(gate_logit) / GLA_GATE_TAU).reshape(bsz, seq, GLA_HEADS, GLA_DK)
        o = gla_chunked(qh, kh, vh, log_a)
        o = o * lax.rsqrt(jnp.mean(o * o, axis=-1, keepdims=True) + NORM_EPS) * gla_norm_g[layer].astype(f32)
        o_gla = o.reshape(bsz, seq, D_GLA).astype(x.dtype) * jax.nn.silu(z_gla)

        y = s5_ssm(u_s5.astype(f32),
                   s5_lambda_re[layer].astype(f32), s5_lambda_im[layer].astype(f32),
                   s5_log_dt[layer].astype(f32),
                   s5_b_re[layer].astype(f32), s5_b_im[layer].astype(f32),
                   s5_c_re[layer].astype(f32), s5_c_im[layer].astype(f32),
                   s5_d[layer].astype(f32))
        y = jax.nn.gelu(y)
        y = y * jax.nn.sigmoid(y @ w_glu[layer].astype(f32) + b_glu[layer].astype(f32))
        o_s5 = y.astype(x.dtype) * jax.nn.silu(z_s5)

        mixed = jnp.concatenate([o_gla, o_s5], axis=-1) @ w_out[layer]
        x = layer_norm(DEEPNORM_ALPHA * x + gate * mixed, ln_g[layer], ln_b[layer])
    return x
```

```python
import functools
import math

import jax
import jax.numpy as jnp
from jax import lax
from jax.experimental import pallas as pl
from jax.experimental.pallas import tpu as pltpu

F32 = jnp.float32
BF16 = jnp.bfloat16

GLA_HEADS = 4
GLA_DK = 256
GLA_DV = 512
GLA_GATE_RANK = 16
GLA_GATE_TAU = 16.0
GLA_CHUNK = 64
S5_GROUP = 16
S5_STATE = 64
NORM_EPS = 1e-5
DEPTH = 1
DEEPNORM_ALPHA = (2.0 * DEPTH) ** 0.25

LANES = 128
SUBLANES = 8
VMEM_LIMIT = 56 * 1024 * 1024

S5_BLOCK_GROUPS = 8
S5_BLOCK_U = S5_BLOCK_GROUPS * S5_GROUP
S5_BLOCK_S = S5_BLOCK_GROUPS * S5_STATE


def _sigmoid(v):
    return 1.0 / (1.0 + jnp.exp(-v))


def _silu(v):
    return v * _sigmoid(v)


def _dot(a, b):
    return jnp.dot(a, b, preferred_element_type=F32)


def _dot_nt(a, b):
    return lax.dot_general(a, b, (((1,), (1,)), ((), ())), preferred_element_type=F32)


def _dot_tn(a, b):
    return lax.dot_general(a, b, (((0,), (0,)), ((), ())), preferred_element_type=F32)


def _params(*sem):
    return pltpu.CompilerParams(dimension_semantics=sem, vmem_limit_bytes=VMEM_LIMIT)


def _ada_kernel(c_ref, w_ref, b_ref, o_ref):
    sc = _silu(c_ref[...])
    o_ref[...] = jnp.dot(sc, w_ref[...], preferred_element_type=F32,
                         precision=lax.Precision.HIGHEST) + b_ref[...]


def _ada(c8, w, b):
    d, n = w.shape
    tn = 512
    return pl.pallas_call(
        _ada_kernel,
        grid=(n // tn,),
        in_specs=[pl.BlockSpec((SUBLANES, d), lambda j: (0, 0)),
                  pl.BlockSpec((d, tn), lambda j: (0, j)),
                  pl.BlockSpec((1, tn), lambda j: (0, j))],
        out_specs=pl.BlockSpec((SUBLANES, tn), lambda j: (0, j)),
        out_shape=jax.ShapeDtypeStruct((SUBLANES, n), F32),
        compiler_params=_params("parallel"),
        name="ada",
    )(c8, w, b)


def _inproj_kernel(x_ref, shift_ref, scale_ref, w_ref, o_ref, h_ref):
    @pl.when(pl.program_id(2) == 0)
    def _():
        h_ref[...] = (x_ref[0] * (1.0 + scale_ref[0]) + shift_ref[0]).astype(BF16)

    o_ref[...] = _dot(h_ref[...], w_ref[...]).astype(o_ref.dtype)


def _inproj(x, mod3, w, *, tn, out_dtype, s5_rows=False, name):
    bsz, seq, d = x.shape
    n = w.shape[1]
    tm = min(512, seq)
    nt = seq // tm
    if s5_rows:
        out_shape = jax.ShapeDtypeStruct((seq, 2 * bsz * tn), out_dtype)
        out_spec = pl.BlockSpec((tm, tn), lambda b, i, j: (i, j * bsz + b))
    else:
        out_shape = jax.ShapeDtypeStruct((bsz * seq, n), out_dtype)
        out_spec = pl.BlockSpec((tm, tn), lambda b, i, j: (b * nt + i, j))
    return pl.pallas_call(
        _inproj_kernel,
        grid=(bsz, nt, n // tn),
        in_specs=[pl.BlockSpec((1, tm, d), lambda b, i, j: (b, i, 0)),
                  pl.BlockSpec((1, 1, d), lambda b, i, j: (b, 0, 0)),
                  pl.BlockSpec((1, 1, d), lambda b, i, j: (b, 0, 1)),
                  pl.BlockSpec((d, tn), lambda b, i, j: (0, j))],
        out_specs=out_spec,
        out_shape=out_shape,
        scratch_shapes=[pltpu.VMEM((tm, d), BF16)],
        compiler_params=_params("parallel", "parallel", "arbitrary"),
        name=name,
    )(x, mod3, mod3, w)


def _gla_kernel(q_ref, k_ref, g_ref, v_ref, z_ref, wg_ref, bg_ref, ng_ref, o_ref, st_ref, *, tc):
    @pl.when(pl.program_id(2) == 0)
    def _():
        st_ref[...] = jnp.zeros_like(st_ref)

    logit = _dot(g_ref[:, :LANES].astype(BF16), wg_ref[...]) + bg_ref[...]
    la = (jnp.minimum(logit, 0.0) - jnp.log(1.0 + jnp.exp(-jnp.abs(logit)))) * (1.0 / GLA_GATE_TAU)
    la_hi = la.astype(BF16)
    la_lo = (la - la_hi.astype(F32)).astype(BF16)
    row = lax.broadcasted_iota(jnp.int32, (tc, tc), 0)
    col = lax.broadcasted_iota(jnp.int32, (tc, tc), 1)
    same = (row // GLA_CHUNK) == (col // GLA_CHUNK)
    tril = jnp.where(same & (col <= row), 1.0, 0.0).astype(BF16)
    ones = jnp.where(same, 1.0, 0.0).astype(BF16)
    b = _dot(tril, la_hi) + _dot(tril, la_lo)
    b_last = _dot(ones, la_hi) + _dot(ones, la_lo)
    k = k_ref[...]
    q_dec = (q_ref[...] * (GLA_DK ** -0.5) * jnp.exp(b)).astype(BF16)
    k_inv = (k * jnp.exp(-b)).astype(BF16)
    k_end = (k * jnp.exp(b_last - b)).astype(BF16)
    decay = jnp.exp(b_last)

    ci = lax.broadcasted_iota(jnp.int32, (GLA_CHUNK, GLA_CHUNK), 0)
    cj = lax.broadcasted_iota(jnp.int32, (GLA_CHUNK, GLA_CHUNK), 1)
    causal = cj <= ci
    for c in range(tc // GLA_CHUNK):
        lo = c * GLA_CHUNK
        qd = q_dec[lo:lo + GLA_CHUNK]
        vc = v_ref[lo:lo + GLA_CHUNK, :]
        att = jnp.where(causal, _dot_nt(qd, k_inv[lo:lo + GLA_CHUNK]), 0.0).astype(BF16)
        st = st_ref[...]
        o = _dot(att, vc) + _dot_nt(qd, st.astype(BF16))
        st_ref[...] = st * decay[lo:lo + 1, :] + _dot_tn(vc, k_end[lo:lo + GLA_CHUNK])
        o = o * lax.rsqrt(jnp.mean(o * o, axis=-1, keepdims=True) + NORM_EPS) * ng_ref[...]
        z = z_ref[lo:lo + GLA_CHUNK, :].astype(F32)
        o_ref[lo:lo + GLA_CHUNK, :] = (o * _silu(z)).astype(o_ref.dtype)


def _gla(qkg, vz, wg, bg, ng, bsz, seq):
    tc = min(256, seq)
    nt = seq // tc
    h = GLA_HEADS
    row = lambda b, hd, i: b * nt + i
    return pl.pallas_call(
        functools.partial(_gla_kernel, tc=tc),
        grid=(bsz, h, nt),
        in_specs=[pl.BlockSpec((tc, GLA_DK), lambda b, hd, i: (row(b, hd, i), hd)),
                  pl.BlockSpec((tc, GLA_DK), lambda b, hd, i: (row(b, hd, i), h + hd)),
                  pl.BlockSpec((tc, GLA_DK), lambda b, hd, i: (row(b, hd, i), 2 * h)),
                  pl.BlockSpec((tc, GLA_DV), lambda b, hd, i: (row(b, hd, i), hd)),
                  pl.BlockSpec((tc, GLA_DV), lambda b, hd, i: (row(b, hd, i), h + hd)),
                  pl.BlockSpec((LANES, GLA_DK), lambda b, hd, i: (0, hd)),
                  pl.BlockSpec((1, GLA_DK), lambda b, hd, i: (0, hd)),
                  pl.BlockSpec((1, GLA_DV), lambda b, hd, i: (0, 0))],
        out_specs=pl.BlockSpec((tc, GLA_DV), lambda b, hd, i: (row(b, hd, i), hd)),
        out_shape=jax.ShapeDtypeStruct((bsz * seq, h * GLA_DV), BF16),
        scratch_shapes=[pltpu.VMEM((GLA_DV, GLA_DK), F32)],
        compiler_params=_params("parallel", "parallel", "arbitrary"),
        name="gla",
    )(qkg, qkg, qkg, vz, vz, wg, bg, ng)


def _s5prep_kernel(lre_ref, lim_ref, ldt_ref, bre_ref, bim_ref, are_ref, aim_ref, bbre_ref, bbim_ref):
    lre = lre_ref[...]
    lim = lim_ref[...]
    dt = jnp.exp(ldt_ref[...])
    z_re = lre * dt
    z_im = lim * dt
    mag = jnp.exp(z_re)
    ab_re = mag * jnp.cos(z_im)
    ab_im = mag * jnp.sin(z_im)
    den = lre * lre + lim * lim
    n_re = ab_re - 1.0
    n_im = ab_im
    f_re = (n_re * lre + n_im * lim) / den
    f_im = (n_im * lre - n_re * lim) / den
    b_re = bre_ref[...]
    b_im = bim_ref[...]
    are_ref[...] = ab_re
    aim_ref[...] = ab_im
    bbre_ref[...] = f_re * b_re - f_im * b_im
    bbim_ref[...] = f_re * b_im + f_im * b_re


def _s5prep(lam_re, lam_im, log_dt, bt_re, bt_im):
    g, p = lam_re.shape
    hh = bt_re.shape[1]
    a_shape = jax.ShapeDtypeStruct((g, 1, p), F32)
    b_shape = jax.ShapeDtypeStruct((g, hh, p), F32)
    return pl.pallas_call(
        _s5prep_kernel,
        out_shape=(a_shape, a_shape, b_shape, b_shape),
        name="s5prep",
    )(lam_re.reshape(g, 1, p), lam_im.reshape(g, 1, p), log_dt.reshape(g, 1, 1), bt_re, bt_im)


def _s5_kernel(u_ref, wb_ref, wc_ref, a_ref, d_ref, o_ref, s_ref, carry_ref, *, tc):
    @pl.when(pl.program_id(1) == 0)
    def _():
        carry_ref[...] = jnp.zeros_like(carry_ref)

    rows = SUBLANES * tc
    uf = u_ref[...].astype(F32)
    lo = (lax.broadcasted_iota(jnp.int32, (rows, S5_BLOCK_U), 0) & 4) == 0
    lhs = jnp.concatenate([jnp.where(lo, uf, 0.0), jnp.where(lo, 0.0, uf)], axis=1).astype(BF16)
    s_ref[...] = _dot(lhs, wb_ref[0])

    a_re = a_ref[0, :, :S5_BLOCK_S]
    a_im = a_ref[0, :, S5_BLOCK_S:]

    def step(t, carry):
        s_re, s_im = carry
        r0 = pl.multiple_of(t * SUBLANES, SUBLANES)
        n_re = a_re * s_re - a_im * s_im + s_ref[pl.ds(r0, SUBLANES), :S5_BLOCK_S]
        n_im = a_re * s_im + a_im * s_re + s_ref[pl.ds(r0, SUBLANES), S5_BLOCK_S:]
        s_ref[pl.ds(r0, SUBLANES), :S5_BLOCK_S] = n_re
        s_ref[pl.ds(r0, SUBLANES), S5_BLOCK_S:] = n_im
        return n_re, n_im

    s_re, s_im = lax.fori_loop(0, tc, step, (carry_ref[:, :S5_BLOCK_S], carry_ref[:, S5_BLOCK_S:]),
                               unroll=8)
    carry_ref[:, :S5_BLOCK_S] = s_re
    carry_ref[:, S5_BLOCK_S:] = s_im

    y8 = _dot(s_ref[...].astype(BF16), wc_ref[0])
    y = jnp.where(lo, y8[:, :S5_BLOCK_U], y8[:, S5_BLOCK_U:])
    y = (y.reshape(tc, SUBLANES, S5_BLOCK_U)
         + d_ref[...] * uf.reshape(tc, SUBLANES, S5_BLOCK_U)).reshape(rows, S5_BLOCK_U)
    cdf = 0.5 * (1.0 + jnp.tanh(math.sqrt(2.0 / math.pi) * (y + 0.044715 * (y * y * y))))
    o_ref[...] = (y * cdf).astype(o_ref.dtype)


def _s5(u2, wb, wc, a8, d8, seq):
    nsb = wb.shape[0]
    tc = min(256, seq)
    rows = SUBLANES * tc
    return pl.pallas_call(
        functools.partial(_s5_kernel, tc=tc),
        grid=(nsb, seq // tc),
        in_specs=[pl.BlockSpec((rows, S5_BLOCK_U), lambda sb, i: (i, sb)),
                  pl.BlockSpec((1, 2 * S5_BLOCK_U, 2 * S5_BLOCK_S), lambda sb, i: (sb, 0, 0)),
                  pl.BlockSpec((1, 2 * S5_BLOCK_S, 2 * S5_BLOCK_U), lambda sb, i: (sb, 0, 0)),
                  pl.BlockSpec((1, SUBLANES, 2 * S5_BLOCK_S), lambda sb, i: (sb, 0, 0)),
                  pl.BlockSpec((1, SUBLANES, S5_BLOCK_U), lambda sb, i: (sb, 0, 0))],
        out_specs=pl.BlockSpec((rows, S5_BLOCK_U), lambda sb, i: (i, sb)),
        out_shape=jax.ShapeDtypeStruct(u2.shape, BF16),
        scratch_shapes=[pltpu.VMEM((rows, 2 * S5_BLOCK_S), F32),
                        pltpu.VMEM((SUBLANES, 2 * S5_BLOCK_S), F32)],
        compiler_params=_params("parallel", "arbitrary"),
        name="s5",
    )(u2, wb, wc, a8, d8)


def _glu_kernel(y0_ref, y1_ref, z_ref, w_ref, b_ref, o_ref):
    y0 = y0_ref[...]
    y1 = y1_ref[...]
    half = y0.shape[1]
    acc = _dot(y0, w_ref[:half, :]) + _dot(y1, w_ref[half:, :]) + b_ref[...]
    y = jnp.concatenate([y0, y1], axis=1).astype(F32)
    z = z_ref[...].astype(F32)
    o_ref[...] = (y * _sigmoid(acc) * _silu(z)).astype(o_ref.dtype)


def _glu(yv, zs, w, b, bsz, seq):
    d = w.shape[0]
    half = d // 2
    tm = min(512, seq)
    nt = seq // tm
    return pl.pallas_call(
        _glu_kernel,
        grid=(bsz, nt),
        in_specs=[pl.BlockSpec((tm, half), lambda b_, i: (i, b_)),
                  pl.BlockSpec((tm, half), lambda b_, i: (i, bsz + b_)),
                  pl.BlockSpec((tm, d), lambda b_, i: (b_ * nt + i, 0)),
                  pl.BlockSpec((d, d), lambda b_, i: (0, 0)),
                  pl.BlockSpec((1, d), lambda b_, i: (0, 0))],
        out_specs=pl.BlockSpec((tm, d), lambda b_, i: (b_ * nt + i, 0)),
        out_shape=jax.ShapeDtypeStruct((bsz * seq, d), BF16),
        compiler_params=_params("parallel", "parallel"),
        name="glu",
    )(yv, yv, zs, w, b)


def _out_kernel(og_ref, os_ref, w_ref, x_ref, gate_ref, lg_ref, lb_ref, o_ref, *, nk):
    k = pl.program_id(2)
    half = nk // 2

    @pl.when(k == 0)
    def _():
        o_ref[0] = _dot(og_ref[...], w_ref[...])

    @pl.when((k > 0) & (k < half))
    def _():
        o_ref[0] += _dot(og_ref[...], w_ref[...])

    @pl.when((k >= half) & (k < nk - 1))
    def _():
        o_ref[0] += _dot(os_ref[...], w_ref[...])

    @pl.when(k == nk - 1)
    def _():
        mixed = o_ref[0] + _dot(os_ref[...], w_ref[...])
        r = DEEPNORM_ALPHA * x_ref[0] + gate_ref[0] * mixed
        mu = jnp.mean(r, axis=-1, keepdims=True)
        rc = r - mu
        var = jnp.mean(rc * rc, axis=-1, keepdims=True)
        o_ref[0] = rc * lax.rsqrt(var + NORM_EPS) * lg_ref[...] + lb_ref[...]


def _outproj(og, osb, w, x, mod3, lg, lb):
    bsz, seq, d = x.shape
    dh = og.shape[1]
    tm = min(512, seq)
    nt = seq // tm
    tk = 512
    half = dh // tk
    nk = 2 * half
    return pl.pallas_call(
        functools.partial(_out_kernel, nk=nk),
        grid=(bsz, nt, nk),
        in_specs=[pl.BlockSpec((tm, tk), lambda b, i, k: (b * nt + i, jnp.minimum(k, half - 1))),
                  pl.BlockSpec((tm, tk), lambda b, i, k: (b * nt + i, jnp.maximum(k - half, 0))),
                  pl.BlockSpec((tk, d), lambda b, i, k: (k, 0)),
                  pl.BlockSpec((1, tm, d), lambda b, i, k: (b, i, 0)),
                  pl.BlockSpec((1, 1, d), lambda b, i, k: (b, 0, 2)),
                  pl.BlockSpec((1, d), lambda b, i, k: (0, 0)),
                  pl.BlockSpec((1, d), lambda b, i, k: (0, 0))],
        out_specs=pl.BlockSpec((1, tm, d), lambda b, i, k: (b, i, 0)),
        out_shape=jax.ShapeDtypeStruct(x.shape, x.dtype),
        compiler_params=_params("parallel", "parallel", "arbitrary"),
        name="outproj",
    )(og, osb, w, x, mod3, lg, lb)


def _s5_layouts(ab_re, ab_im, bbt_re, bbt_im, c_re, c_im, d_skip, bsz):
    g = ab_re.shape[0]
    nsb = g // (2 * S5_BLOCK_GROUPS)
    eye = jnp.eye(S5_BLOCK_GROUPS, dtype=F32)

    def in_map(bbt):
        t = bbt.reshape(2, nsb, S5_BLOCK_GROUPS, S5_GROUP, S5_STATE)
        t = jnp.einsum('fsihp,ij->sfihjp', t, eye)
        return t.reshape(nsb, 2 * S5_BLOCK_U, S5_BLOCK_S)

    def out_map(c):
        t = c.reshape(2, nsb, S5_BLOCK_GROUPS, S5_GROUP, S5_STATE)
        t = jnp.einsum('fsjhp,ij->sipfjh', t, eye)
        return t.reshape(nsb, S5_BLOCK_S, 2 * S5_BLOCK_U)

    def rows8(v, width):
        t = v.reshape(2, nsb, 1, width).transpose(1, 0, 2, 3)
        return jnp.broadcast_to(t, (nsb, 2, bsz, width)).reshape(nsb, 2 * bsz, width)

    wb = jnp.concatenate([in_map(bbt_re), in_map(bbt_im)], axis=2).astype(BF16)
    wc = jnp.concatenate([out_map(c_re), -out_map(c_im)], axis=1).astype(BF16)
    a8 = jnp.concatenate([rows8(ab_re.reshape(-1), S5_BLOCK_S), rows8(ab_im.reshape(-1), S5_BLOCK_S)], axis=2)
    d8 = rows8(d_skip, S5_BLOCK_U)
    return wb, wc, a8, d8


def kernel(x, c, w_ada, b_ada, w_in, w_gla_gate, b_gla_gate, gla_norm_g, s5_lambda_re, s5_lambda_im, s5_log_dt, s5_b_re, s5_b_im, s5_c_re, s5_c_im, s5_d, w_glu, b_glu, w_out, ln_g, ln_b):
    bsz, seq, d = x.shape
    assert bsz * 2 == SUBLANES and w_ada.shape[0] == DEPTH
    dk_tot = GLA_HEADS * GLA_DK
    d_gla = GLA_HEADS * GLA_DV
    d_s5 = s5_d.shape[1]
    layer = 0

    c8 = jnp.pad(c, ((0, SUBLANES - bsz), (0, 0)))
    mod = _ada(c8, w_ada[layer], b_ada[layer][None, :])[:bsz]
    mod3 = mod.reshape(bsz, 1, 3 * d)

    w = w_in[layer]
    o_q, o_k, o_v = 0, dk_tot, 2 * dk_tot
    o_g = o_v + d_gla
    o_zg = o_g + GLA_GATE_RANK
    o_u = o_zg + d_gla
    o_zs = o_u + d_s5
    w_a = jnp.concatenate([w[:, o_q:o_v], jnp.pad(w[:, o_g:o_zg], ((0, 0), (0, GLA_DK - GLA_GATE_RANK)))],
                          axis=1).astype(BF16)
    w_b = jnp.concatenate([w[:, o_v:o_g], w[:, o_zg:o_u]], axis=1).astype(BF16)
    w_c = w[:, o_u:o_zs].astype(BF16)
    w_d = w[:, o_zs:].astype(BF16)
    qkg = _inproj(x, mod3, w_a, tn=(2 * dk_tot + GLA_DK) // 2, out_dtype=F32, name="inproj_qkg")
    vz = _inproj(x, mod3, w_b, tn=1024, out_dtype=BF16, name="inproj_vz")
    u_v = _inproj(x, mod3, w_c, tn=d_s5 // 2, out_dtype=BF16, s5_rows=True, name="inproj_u")
    zs = _inproj(x, mod3, w_d, tn=1024, out_dtype=BF16, name="inproj_zs")

    wg = jnp.pad(w_gla_gate[layer], ((0, LANES - GLA_GATE_RANK), (0, 0))).astype(BF16)
    o_gla = _gla(qkg, vz, wg, b_gla_gate[layer][None, :], gla_norm_g[layer][None, :], bsz, seq)

    bt_re = s5_b_re[layer].transpose(0, 2, 1)
    bt_im = s5_b_im[layer].transpose(0, 2, 1)
    ab_re, ab_im, bbt_re, bbt_im = _s5prep(s5_lambda_re[layer], s5_lambda_im[layer], s5_log_dt[layer],
                                            bt_re, bt_im)
    wb, wc, a8, d8 = _s5_layouts(ab_re, ab_im, bbt_re, bbt_im, s5_c_re[layer], s5_c_im[layer],
                                 s5_d[layer], bsz)
    u2 = u_v.reshape(seq * SUBLANES, d_s5 // 2)
    y2 = _s5(u2, wb, wc, a8, d8, seq)
    o_s5 = _glu(y2.reshape(seq, SUBLANES * (d_s5 // 2)), zs, w_glu[layer].astype(BF16),
                b_glu[layer][None, :], bsz, seq)

    return _outproj(o_gla, o_s5, w_out[layer].astype(BF16), x, mod3,
                    ln_g[layer][None, :], ln_b[layer][None, :])
```

```python
import functools
import math

import jax
import jax.numpy as jnp
from jax import lax
from jax.experimental import pallas as pl
from jax.experimental.pallas import tpu as pltpu

F32 = jnp.float32
BF16 = jnp.bfloat16

GLA_HEADS = 4
GLA_DK = 256
GLA_DV = 512
GLA_GATE_RANK = 16
GLA_GATE_TAU = 16.0
GLA_CHUNK = 64
S5_GROUP = 16
S5_STATE = 64
NORM_EPS = 1e-5
DEPTH = 1
DEEPNORM_ALPHA = (2.0 * DEPTH) ** 0.25

LANES = 128
SUBLANES = 8
VMEM_LIMIT = 56 * 1024 * 1024

DK_TOT = GLA_HEADS * GLA_DK
D_GLA = GLA_HEADS * GLA_DV
D_S5 = 2048
COL_Q = 0
COL_K = COL_Q + DK_TOT
COL_V = COL_K + DK_TOT
COL_ZG = COL_V + D_GLA
COL_U = COL_ZG + D_GLA
COL_ZS = COL_U + D_S5
COL_G = COL_ZS + D_S5
N_PROJ = COL_G + LANES
INPROJ_TN = N_PROJ // 9

S5_BLOCK_GROUPS = 8
S5_BLOCK_U = S5_BLOCK_GROUPS * S5_GROUP
S5_BLOCK_S = S5_BLOCK_GROUPS * S5_STATE
S5_SUB = 32


def _sigmoid(v):
    return 1.0 / (1.0 + jnp.exp(-v))


def _silu(v):
    return v * _sigmoid(v)


def _dot(a, b):
    return jnp.dot(a, b, preferred_element_type=F32)


def _dot_nt(a, b):
    return lax.dot_general(a, b, (((1,), (1,)), ((), ())), preferred_element_type=F32)


def _dot_tn(a, b):
    return lax.dot_general(a, b, (((0,), (0,)), ((), ())), preferred_element_type=F32)


def _params(*sem):
    return pltpu.CompilerParams(dimension_semantics=sem, vmem_limit_bytes=VMEM_LIMIT)


def _ada_kernel(c_ref, w_ref, b_ref, o_ref):
    sc = _silu(c_ref[...])
    o_ref[...] = jnp.dot(sc, w_ref[...], preferred_element_type=F32,
                         precision=lax.Precision.HIGHEST) + b_ref[...]


def _ada(c8, w, b):
    d, n = w.shape
    tn = 512
    return pl.pallas_call(
        _ada_kernel,
        grid=(n // tn,),
        in_specs=[pl.BlockSpec((SUBLANES, d), lambda j: (0, 0)),
                  pl.BlockSpec((d, tn), lambda j: (0, j)),
                  pl.BlockSpec((1, tn), lambda j: (0, j))],
        out_specs=pl.BlockSpec((SUBLANES, tn), lambda j: (0, j)),
        out_shape=jax.ShapeDtypeStruct((SUBLANES, n), F32),
        compiler_params=_params("parallel"),
        name="ada",
    )(c8, w, b)


def _inproj_kernel(x_ref, shift_ref, scale_ref, w_ref, o_ref, h_ref):
    @pl.when(pl.program_id(2) == 0)
    def _():
        h_ref[...] = (x_ref[0] * (1.0 + scale_ref[0]) + shift_ref[0]).astype(BF16)

    o_ref[...] = _dot(h_ref[...], w_ref[...]).astype(o_ref.dtype)


def _inproj(x, mod3, w):
    bsz, seq, d = x.shape
    n = w.shape[1]
    tn = INPROJ_TN
    tm = min(512, seq)
    nt = seq // tm
    return pl.pallas_call(
        _inproj_kernel,
        grid=(bsz, nt, n // tn),
        in_specs=[pl.BlockSpec((1, tm, d), lambda b, i, j: (b, i, 0)),
                  pl.BlockSpec((1, 1, d), lambda b, i, j: (b, 0, 0)),
                  pl.BlockSpec((1, 1, d), lambda b, i, j: (b, 0, 1)),
                  pl.BlockSpec((d, tn), lambda b, i, j: (0, j))],
        out_specs=pl.BlockSpec((tm, tn), lambda b, i, j: (b * nt + i, j)),
        out_shape=jax.ShapeDtypeStruct((bsz * seq, n), BF16),
        scratch_shapes=[pltpu.VMEM((tm, d), BF16)],
        compiler_params=_params("parallel", "parallel", "arbitrary"),
        name="inproj",
    )(x, mod3, mod3, w)


def _gla_kernel(q_ref, k_ref, g_ref, v_ref, z_ref, wg_ref, bg_ref, ng_ref, o_ref, st_ref, *, tc):
    @pl.when(pl.program_id(2) == 0)
    def _():
        st_ref[...] = jnp.zeros_like(st_ref)

    logit = _dot(g_ref[...], wg_ref[...]) + bg_ref[...]
    la = (jnp.minimum(logit, 0.0) - jnp.log(1.0 + jnp.exp(-jnp.abs(logit)))) * (1.0 / GLA_GATE_TAU)
    la_hi = la.astype(BF16)
    la_lo = (la - la_hi.astype(F32)).astype(BF16)
    row = lax.broadcasted_iota(jnp.int32, (tc, tc), 0)
    col = lax.broadcasted_iota(jnp.int32, (tc, tc), 1)
    same = (row // GLA_CHUNK) == (col // GLA_CHUNK)
    tril = jnp.where(same & (col <= row), 1.0, 0.0).astype(BF16)
    ones = jnp.where(same, 1.0, 0.0).astype(BF16)
    b = _dot(tril, la_hi) + _dot(tril, la_lo)
    b_last = _dot(ones, la_hi) + _dot(ones, la_lo)
    k = k_ref[...].astype(F32)
    q_dec = (q_ref[...].astype(F32) * (GLA_DK ** -0.5) * jnp.exp(b)).astype(BF16)
    k_inv = (k * jnp.exp(-b)).astype(BF16)
    k_end = (k * jnp.exp(b_last - b)).astype(BF16)
    decay = jnp.exp(b_last)

    ci = lax.broadcasted_iota(jnp.int32, (GLA_CHUNK, GLA_CHUNK), 0)
    cj = lax.broadcasted_iota(jnp.int32, (GLA_CHUNK, GLA_CHUNK), 1)
    causal = cj <= ci
    for c in range(tc // GLA_CHUNK):
        lo = c * GLA_CHUNK
        qd = q_dec[lo:lo + GLA_CHUNK]
        vc = v_ref[lo:lo + GLA_CHUNK, :]
        att = jnp.where(causal, _dot_nt(qd, k_inv[lo:lo + GLA_CHUNK]), 0.0).astype(BF16)
        st = st_ref[...]
        o = _dot(att, vc) + _dot_nt(qd, st.astype(BF16))
        st_ref[...] = st * decay[lo:lo + 1, :] + _dot_tn(vc, k_end[lo:lo + GLA_CHUNK])
        o = o * lax.rsqrt(jnp.mean(o * o, axis=-1, keepdims=True) + NORM_EPS) * ng_ref[...]
        z = z_ref[lo:lo + GLA_CHUNK, :].astype(F32)
        o_ref[lo:lo + GLA_CHUNK, :] = (o * _silu(z)).astype(o_ref.dtype)


def _gla(proj, wg, bg, ng, bsz, seq):
    tc = min(256, seq)
    nt = seq // tc
    h = GLA_HEADS
    row = lambda b, hd, i: b * nt + i
    return pl.pallas_call(
        functools.partial(_gla_kernel, tc=tc),
        grid=(bsz, h, nt),
        in_specs=[pl.BlockSpec((tc, GLA_DK), lambda b, hd, i: (row(b, hd, i), COL_Q // GLA_DK + hd)),
                  pl.BlockSpec((tc, GLA_DK), lambda b, hd, i: (row(b, hd, i), COL_K // GLA_DK + hd)),
                  pl.BlockSpec((tc, LANES), lambda b, hd, i: (row(b, hd, i), COL_G // LANES)),
                  pl.BlockSpec((tc, GLA_DV), lambda b, hd, i: (row(b, hd, i), COL_V // GLA_DV + hd)),
                  pl.BlockSpec((tc, GLA_DV), lambda b, hd, i: (row(b, hd, i), COL_ZG // GLA_DV + hd)),
                  pl.BlockSpec((LANES, GLA_DK), lambda b, hd, i: (0, hd)),
                  pl.BlockSpec((1, GLA_DK), lambda b, hd, i: (0, hd)),
                  pl.BlockSpec((1, GLA_DV), lambda b, hd, i: (0, 0))],
        out_specs=pl.BlockSpec((tc, GLA_DV), lambda b, hd, i: (row(b, hd, i), hd)),
        out_shape=jax.ShapeDtypeStruct((bsz * seq, h * GLA_DV), BF16),
        scratch_shapes=[pltpu.VMEM((GLA_DV, GLA_DK), F32)],
        compiler_params=_params("parallel", "parallel", "arbitrary"),
        name="gla",
    )(proj, proj, proj, proj, proj, wg, bg, ng)


def _s5prep_kernel(lre_ref, lim_ref, ldt_ref, bre_ref, bim_ref, are_ref, aim_ref, bbre_ref, bbim_ref):
    lre = lre_ref[...]
    lim = lim_ref[...]
    dt = jnp.exp(ldt_ref[...])
    z_re = lre * dt
    z_im = lim * dt
    mag = jnp.exp(z_re)
    ab_re = mag * jnp.cos(z_im)
    ab_im = mag * jnp.sin(z_im)
    den = lre * lre + lim * lim
    n_re = ab_re - 1.0
    n_im = ab_im
    f_re = (n_re * lre + n_im * lim) / den
    f_im = (n_im * lre - n_re * lim) / den
    b_re = bre_ref[...]
    b_im = bim_ref[...]
    are_ref[...] = ab_re
    aim_ref[...] = ab_im
    bbre_ref[...] = f_re * b_re - f_im * b_im
    bbim_ref[...] = f_re * b_im + f_im * b_re


def _s5prep(lam_re, lam_im, log_dt, bt_re, bt_im):
    g, p = lam_re.shape
    hh = bt_re.shape[1]
    a_shape = jax.ShapeDtypeStruct((g, 1, p), F32)
    b_shape = jax.ShapeDtypeStruct((g, hh, p), F32)
    return pl.pallas_call(
        _s5prep_kernel,
        out_shape=(a_shape, a_shape, b_shape, b_shape),
        name="s5prep",
    )(lam_re.reshape(g, 1, p), lam_im.reshape(g, 1, p), log_dt.reshape(g, 1, 1), bt_re, bt_im)


def _s5_kernel(u0_ref, u1_ref, wb_ref, wc_ref, a_ref, d_ref, o_ref, uf_ref, bu_ref, s_ref, y_ref, carry_ref, *, tc):
    @pl.when(pl.program_id(1) == 0)
    def _():
        carry_ref[...] = jnp.zeros_like(carry_ref)

    nb = u0_ref.shape[0]
    tile = SUBLANES * S5_SUB
    for b in range(nb):
        uf_ref[pl.ds(b, tc, stride=SUBLANES), :] = u0_ref[b].astype(F32)
        uf_ref[pl.ds(nb + b, tc, stride=SUBLANES), :] = u1_ref[b].astype(F32)

    lo = (lax.broadcasted_iota(jnp.int32, (tile, S5_BLOCK_U), 0) & nb) == 0
    a_re = a_ref[0, :, :S5_BLOCK_S]
    a_im = a_ref[0, :, S5_BLOCK_S:]
    s_re = carry_ref[:, :S5_BLOCK_S]
    s_im = carry_ref[:, S5_BLOCK_S:]

    for s in range(tc // S5_SUB):
        base = s * tile
        uf = uf_ref[base:base + tile, :]
        lhs = jnp.concatenate([jnp.where(lo, uf, 0.0), jnp.where(lo, 0.0, uf)], axis=1).astype(BF16)
        bu_ref[base:base + tile, :] = _dot(lhs, wb_ref[0])
        for p in range(S5_SUB // 2):
            r0 = base + p * 2 * SUBLANES
            r1 = r0 + SUBLANES
            m_re = a_re * s_re - a_im * s_im + bu_ref[r0:r1, :S5_BLOCK_S]
            m_im = a_re * s_im + a_im * s_re + bu_ref[r0:r1, S5_BLOCK_S:]
            s_re = a_re * m_re - a_im * m_im + bu_ref[r1:r1 + SUBLANES, :S5_BLOCK_S]
            s_im = a_re * m_im + a_im * m_re + bu_ref[r1:r1 + SUBLANES, S5_BLOCK_S:]
            s_ref[r0:r0 + 2 * SUBLANES, :S5_BLOCK_S] = jnp.concatenate([m_re, s_re], axis=0).astype(BF16)
            s_ref[r0:r0 + 2 * SUBLANES, S5_BLOCK_S:] = jnp.concatenate([m_im, s_im], axis=0).astype(BF16)
        y8 = _dot(s_ref[base:base + tile, :], wc_ref[0])
        y = jnp.where(lo, y8[:, :S5_BLOCK_U], y8[:, S5_BLOCK_U:])
        y = (y.reshape(S5_SUB, SUBLANES, S5_BLOCK_U)
             + d_ref[...] * uf.reshape(S5_SUB, SUBLANES, S5_BLOCK_U)).reshape(tile, S5_BLOCK_U)
        cdf = 0.5 * (1.0 + jnp.tanh(math.sqrt(2.0 / math.pi) * (y + 0.044715 * (y * y * y))))
        y_ref[base:base + tile, :] = y * cdf

    carry_ref[:, :S5_BLOCK_S] = s_re
    carry_ref[:, S5_BLOCK_S:] = s_im
    for j in range(2 * nb):
        o_ref[j] = y_ref[pl.ds(j, tc, stride=SUBLANES), :].astype(o_ref.dtype)


def _s5(proj3, wb, wc, a8, d8):
    bsz, seq, _ = proj3.shape
    nsb = wb.shape[0]
    tc = min(256, seq)
    rows = SUBLANES * tc
    cb = COL_U // S5_BLOCK_U
    return pl.pallas_call(
        functools.partial(_s5_kernel, tc=tc),
        grid=(nsb, seq // tc),
        in_specs=[pl.BlockSpec((bsz, tc, S5_BLOCK_U), lambda sb, i: (0, i, cb + sb)),
                  pl.BlockSpec((bsz, tc, S5_BLOCK_U), lambda sb, i: (0, i, cb + nsb + sb)),
                  pl.BlockSpec((1, 2 * S5_BLOCK_U, 2 * S5_BLOCK_S), lambda sb, i: (sb, 0, 0)),
                  pl.BlockSpec((1, 2 * S5_BLOCK_S, 2 * S5_BLOCK_U), lambda sb, i: (sb, 0, 0)),
                  pl.BlockSpec((1, SUBLANES, 2 * S5_BLOCK_S), lambda sb, i: (sb, 0, 0)),
                  pl.BlockSpec((1, SUBLANES, S5_BLOCK_U), lambda sb, i: (sb, 0, 0))],
        out_specs=pl.BlockSpec((2 * bsz, tc, S5_BLOCK_U), lambda sb, i: (0, i, sb)),
        out_shape=jax.ShapeDtypeStruct((2 * bsz, seq, nsb * S5_BLOCK_U), BF16),
        scratch_shapes=[pltpu.VMEM((rows, S5_BLOCK_U), F32),
                        pltpu.VMEM((rows, 2 * S5_BLOCK_S), F32),
                        pltpu.VMEM((rows, 2 * S5_BLOCK_S), BF16),
                        pltpu.VMEM((rows, S5_BLOCK_U), F32),
                        pltpu.VMEM((SUBLANES, 2 * S5_BLOCK_S), F32)],
        compiler_params=_params("parallel", "arbitrary"),
        name="s5",
    )(proj3, proj3, wb, wc, a8, d8)


def _glu_kernel(y0_ref, y1_ref, z_ref, w_ref, b_ref, o_ref):
    y0 = y0_ref[0]
    y1 = y1_ref[0]
    half = y0.shape[1]
    acc = _dot(y0, w_ref[:half, :]) + _dot(y1, w_ref[half:, :]) + b_ref[...]
    y = jnp.concatenate([y0, y1], axis=1).astype(F32)
    z = z_ref[...].astype(F32)
    o_ref[...] = (y * _sigmoid(acc) * _silu(z)).astype(o_ref.dtype)


def _glu(yv, proj, w, b, bsz, seq):
    d = w.shape[0]
    half = d // 2
    tm = min(512, seq)
    nt = seq // tm
    return pl.pallas_call(
        _glu_kernel,
        grid=(bsz, nt),
        in_specs=[pl.BlockSpec((1, tm, half), lambda b_, i: (b_, i, 0)),
                  pl.BlockSpec((1, tm, half), lambda b_, i: (bsz + b_, i, 0)),
                  pl.BlockSpec((tm, d), lambda b_, i: (b_ * nt + i, COL_ZS // D_S5)),
                  pl.BlockSpec((d, d), lambda b_, i: (0, 0)),
                  pl.BlockSpec((1, d), lambda b_, i: (0, 0))],
        out_specs=pl.BlockSpec((tm, d), lambda b_, i: (b_ * nt + i, 0)),
        out_shape=jax.ShapeDtypeStruct((bsz * seq, d), BF16),
        compiler_params=_params("parallel", "parallel"),
        name="glu",
    )(yv, yv, proj, w, b)


def _out_kernel(og_ref, os_ref, w_ref, x_ref, gate_ref, lg_ref, lb_ref, o_ref, *, nk):
    k = pl.program_id(2)
    half = nk // 2

    @pl.when(k == 0)
    def _():
        o_ref[0] = _dot(og_ref[...], w_ref[...])

    @pl.when((k > 0) & (k < half))
    def _():
        o_ref[0] += _dot(og_ref[...], w_ref[...])

    @pl.when((k >= half) & (k < nk - 1))
    def _():
        o_ref[0] += _dot(os_ref[...], w_ref[...])

    @pl.when(k == nk - 1)
    def _():
        mixed = o_ref[0] + _dot(os_ref[...], w_ref[...])
        r = DEEPNORM_ALPHA * x_ref[0] + gate_ref[0] * mixed
        mu = jnp.mean(r, axis=-1, keepdims=True)
        rc = r - mu
        var = jnp.mean(rc * rc, axis=-1, keepdims=True)
        o_ref[0] = rc * lax.rsqrt(var + NORM_EPS) * lg_ref[...] + lb_ref[...]


def _outproj(og, osb, w, x, mod3, lg, lb):
    bsz, seq, d = x.shape
    dh = og.shape[1]
    tm = min(512, seq)
    nt = seq // tm
    tk = 512
    half = dh // tk
    nk = 2 * half
    return pl.pallas_call(
        functools.partial(_out_kernel, nk=nk),
        grid=(bsz, nt, nk),
        in_specs=[pl.BlockSpec((tm, tk), lambda b, i, k: (b * nt + i, jnp.minimum(k, half - 1))),
                  pl.BlockSpec((tm, tk), lambda b, i, k: (b * nt + i, jnp.maximum(k - half, 0))),
                  pl.BlockSpec((tk, d), lambda b, i, k: (k, 0)),
                  pl.BlockSpec((1, tm, d), lambda b, i, k: (b, i, 0)),
                  pl.BlockSpec((1, 1, d), lambda b, i, k: (b, 0, 2)),
                  pl.BlockSpec((1, d), lambda b, i, k: (0, 0)),
                  pl.BlockSpec((1, d), lambda b, i, k: (0, 0))],
        out_specs=pl.BlockSpec((1, tm, d), lambda b, i, k: (b, i, 0)),
        out_shape=jax.ShapeDtypeStruct(x.shape, x.dtype),
        compiler_params=_params("parallel", "parallel", "arbitrary"),
        name="outproj",
    )(og, osb, w, x, mod3, lg, lb)


def _s5_layouts(ab_re, ab_im, bbt_re, bbt_im, c_re, c_im, d_skip, bsz):
    g = ab_re.shape[0]
    nsb = g // (2 * S5_BLOCK_GROUPS)
    eye = jnp.eye(S5_BLOCK_GROUPS, dtype=F32)

    def in_map(bbt):
        t = bbt.reshape(2, nsb, S5_BLOCK_GROUPS, S5_GROUP, S5_STATE)
        t = jnp.einsum('fsihp,ij->sfihjp', t, eye)
        return t.reshape(nsb, 2 * S5_BLOCK_U, S5_BLOCK_S)

    def out_map(c):
        t = c.reshape(2, nsb, S5_BLOCK_GROUPS, S5_GROUP, S5_STATE)
        t = jnp.einsum('fsjhp,ij->sipfjh', t, eye)
        return t.reshape(nsb, S5_BLOCK_S, 2 * S5_BLOCK_U)

    def rows8(v, width):
        t = v.reshape(2, nsb, 1, width).transpose(1, 0, 2, 3)
        return jnp.broadcast_to(t, (nsb, 2, bsz, width)).reshape(nsb, 2 * bsz, width)

    wb = jnp.concatenate([in_map(bbt_re), in_map(bbt_im)], axis=2).astype(BF16)
    wc = jnp.concatenate([out_map(c_re), -out_map(c_im)], axis=1).astype(BF16)
    a8 = jnp.concatenate([rows8(ab_re.reshape(-1), S5_BLOCK_S), rows8(ab_im.reshape(-1), S5_BLOCK_S)], axis=2)
    d8 = rows8(d_skip, S5_BLOCK_U)
    return wb, wc, a8, d8


def kernel(x, c, w_ada, b_ada, w_in, w_gla_gate, b_gla_gate, gla_norm_g, s5_lambda_re, s5_lambda_im, s5_log_dt, s5_b_re, s5_b_im, s5_c_re, s5_c_im, s5_d, w_glu, b_glu, w_out, ln_g, ln_b):
    bsz, seq, d = x.shape
    assert bsz * 2 == SUBLANES and w_ada.shape[0] == DEPTH and s5_d.shape[1] == D_S5
    layer = 0

    c8 = jnp.pad(c, ((0, SUBLANES - bsz), (0, 0)))
    mod = _ada(c8, w_ada[layer], b_ada[layer][None, :])[:bsz]
    mod3 = mod.reshape(bsz, 1, 3 * d)

    w = w_in[layer]
    o_g = 2 * DK_TOT + D_GLA
    w_all = jnp.concatenate([w[:, :o_g], w[:, o_g + GLA_GATE_RANK:],
                             jnp.pad(w[:, o_g:o_g + GLA_GATE_RANK], ((0, 0), (0, LANES - GLA_GATE_RANK)))],
                            axis=1).astype(BF16)
    proj = _inproj(x, mod3, w_all)

    wg = jnp.pad(w_gla_gate[layer], ((0, LANES - GLA_GATE_RANK), (0, 0))).astype(BF16)
    o_gla = _gla(proj, wg, b_gla_gate[layer][None, :], gla_norm_g[layer][None, :], bsz, seq)

    bt_re = s5_b_re[layer].transpose(0, 2, 1)
    bt_im = s5_b_im[layer].transpose(0, 2, 1)
    ab_re, ab_im, bbt_re, bbt_im = _s5prep(s5_lambda_re[layer], s5_lambda_im[layer], s5_log_dt[layer],
                                            bt_re, bt_im)
    wb, wc, a8, d8 = _s5_layouts(ab_re, ab_im, bbt_re, bbt_im, s5_c_re[layer], s5_c_im[layer],
                                 s5_d[layer], bsz)
    yv = _s5(proj.reshape(bsz, seq, N_PROJ), wb, wc, a8, d8)
    o_s5 = _glu(yv, proj, w_glu[layer].astype(BF16), b_glu[layer][None, :], bsz, seq)

    return _outproj(o_gla, o_s5, w_out[layer].astype(BF16), x, mod3,
                    ln_g[layer][None, :], ln_b[layer][None, :])
```

```python
import functools
import math

import jax
import jax.numpy as jnp
from jax import lax
from jax.experimental import pallas as pl
from jax.experimental.pallas import tpu as pltpu

F32 = jnp.float32
BF16 = jnp.bfloat16

GLA_HEADS = 4
GLA_DK = 256
GLA_DV = 512
GLA_GATE_RANK = 16
GLA_GATE_TAU = 16.0
GLA_CHUNK = 64
S5_GROUP = 16
S5_STATE = 64
NORM_EPS = 1e-5
DEPTH = 1
DEEPNORM_ALPHA = (2.0 * DEPTH) ** 0.25

LANES = 128
SUBLANES = 8
VMEM_LIMIT = 56 * 1024 * 1024

DK_TOT = GLA_HEADS * GLA_DK
D_GLA = GLA_HEADS * GLA_DV
D_S5 = 2048
COL_Q = 0
COL_K = COL_Q + DK_TOT
COL_V = COL_K + DK_TOT
COL_ZG = COL_V + D_GLA
COL_U = COL_ZG + D_GLA
COL_ZS = COL_U + D_S5
N_PROJ = COL_ZS + D_S5
MXU_WIDTH = 256
INPROJ_TN = 5 * MXU_WIDTH

S5_BLOCK_GROUPS = 8
S5_BLOCK_U = S5_BLOCK_GROUPS * S5_GROUP
S5_BLOCK_S = S5_BLOCK_GROUPS * S5_STATE
S5_SUB = 32


def _sigmoid(v):
    return 1.0 / (1.0 + jnp.exp(-v))


def _silu(v):
    return v * _sigmoid(v)


def _dot(a, b):
    return jnp.dot(a, b, preferred_element_type=F32)


def _dot_nt(a, b):
    return lax.dot_general(a, b, (((1,), (1,)), ((), ())), preferred_element_type=F32)


def _dot_tn(a, b):
    return lax.dot_general(a, b, (((0,), (0,)), ((), ())), preferred_element_type=F32)


def _params(*sem):
    return pltpu.CompilerParams(dimension_semantics=sem, vmem_limit_bytes=VMEM_LIMIT)


def _ada_kernel(c_ref, w_ref, b_ref, o_ref):
    sc = _silu(c_ref[...])
    o_ref[...] = jnp.dot(sc, w_ref[...], preferred_element_type=F32,
                         precision=lax.Precision.HIGHEST) + b_ref[...]


def _ada(c8, w, b):
    d, n = w.shape
    tn = 512
    return pl.pallas_call(
        _ada_kernel,
        grid=(n // tn,),
        in_specs=[pl.BlockSpec((SUBLANES, d), lambda j: (0, 0)),
                  pl.BlockSpec((d, tn), lambda j: (0, j)),
                  pl.BlockSpec((1, tn), lambda j: (0, j))],
        out_specs=pl.BlockSpec((SUBLANES, tn), lambda j: (0, j)),
        out_shape=jax.ShapeDtypeStruct((SUBLANES, n), F32),
        compiler_params=_params("parallel"),
        name="ada",
    )(c8, w, b)


def _inproj_kernel(x_ref, shift_ref, scale_ref, w_ref, wg_ref, o_ref, g_ref, h_ref):
    @pl.when(pl.program_id(2) == 0)
    def _():
        h_ref[...] = (x_ref[0] * (1.0 + scale_ref[0]) + shift_ref[0]).astype(BF16)
        g_ref[...] = _dot(h_ref[...], wg_ref[...]).astype(g_ref.dtype)

    o_ref[...] = _dot(h_ref[...], w_ref[...]).astype(o_ref.dtype)


def _inproj(x, mod3, w, wg):
    bsz, seq, d = x.shape
    n = w.shape[1]
    tn = INPROJ_TN
    tm = min(512, seq)
    nt = seq // tm
    return pl.pallas_call(
        _inproj_kernel,
        grid=(bsz, nt, n // tn),
        in_specs=[pl.BlockSpec((1, tm, d), lambda b, i, j: (b, i, 0)),
                  pl.BlockSpec((1, 1, d), lambda b, i, j: (b, 0, 0)),
                  pl.BlockSpec((1, 1, d), lambda b, i, j: (b, 0, 1)),
                  pl.BlockSpec((d, tn), lambda b, i, j: (0, j)),
                  pl.BlockSpec((d, LANES), lambda b, i, j: (0, 0))],
        out_specs=(pl.BlockSpec((tm, tn), lambda b, i, j: (b * nt + i, j)),
                   pl.BlockSpec((tm, LANES), lambda b, i, j: (b * nt + i, 0))),
        out_shape=(jax.ShapeDtypeStruct((bsz * seq, n), BF16),
                   jax.ShapeDtypeStruct((bsz * seq, LANES), BF16)),
        scratch_shapes=[pltpu.VMEM((tm, d), BF16)],
        compiler_params=_params("parallel", "parallel", "arbitrary"),
        name="inproj",
    )(x, mod3, mod3, w, wg)


GLA_BLOCK = 4 * GLA_CHUNK
GLA_SUMS = 5


def _gla_constants():
    import numpy as np
    tc = GLA_BLOCK
    row = np.arange(tc)[:, None]
    col = np.arange(tc)[None, :]
    rc = row // GLA_CHUNK
    cc = col // GLA_CHUNK
    half = tc // GLA_CHUNK // 2
    same = rc == cc
    sums = [same & (col <= row),
            same,
            cc < rc,
            cc > rc,
            ((cc >= half) & (cc < rc)) | ((cc > rc) & (cc < half))]
    sel = np.where(same & (col <= row), 1.0,
                   np.where((rc == cc + 1) & (rc != half), 2.0,
                            np.where((rc >= half) & (cc < half), 3.0, 0.0)))
    return (jnp.asarray(np.concatenate(sums, axis=0).astype(np.float32), dtype=BF16),
            jnp.asarray(sel.astype(np.float32)))


def _gla_kernel(q_ref, k_ref, g_ref, v_ref, z_ref, wg_ref, bg_ref, ng_ref, sums_ref, sel_ref, o_ref, st_ref):
    @pl.when(pl.program_id(1) == 0)
    def _():
        st_ref[...] = jnp.zeros_like(st_ref)

    tc = GLA_BLOCK
    logit = _dot(g_ref[...], wg_ref[...]) + bg_ref[...]
    la_all = (jnp.minimum(logit, 0.0) - jnp.log(1.0 + jnp.exp(-jnp.abs(logit)))) * (1.0 / GLA_GATE_TAU)
    sel = sel_ref[...]
    for hd in range(GLA_HEADS):
        ck = slice(hd * GLA_DK, (hd + 1) * GLA_DK)
        cv = slice(hd * GLA_DV, (hd + 1) * GLA_DV)
        la = la_all[:, ck]
        la_hi = la.astype(BF16)
        la_lo = (la - la_hi.astype(F32)).astype(BF16)
        sums = _dot(sums_ref[...], la_hi) + _dot(sums_ref[...], la_lo)
        b, b_last, pre, suf, mid = [sums[i * tc:(i + 1) * tc] for i in range(GLA_SUMS)]

        k = k_ref[:, ck].astype(F32)
        q_dec = q_ref[:, ck].astype(F32) * (GLA_DK ** -0.5) * jnp.exp(b)
        k_end = k * jnp.exp(b_last - b)
        e_mid = jnp.exp(mid)
        a_same = _dot_nt(q_dec.astype(BF16), (k * jnp.exp(-b)).astype(BF16))
        a_next = _dot_nt(q_dec.astype(BF16), k_end.astype(BF16))
        a_mid = _dot_nt((q_dec * e_mid).astype(BF16), (k_end * e_mid).astype(BF16))
        att = jnp.where(sel == 1.0, a_same,
                        jnp.where(sel == 2.0, a_next, jnp.where(sel == 3.0, a_mid, 0.0))).astype(BF16)
        v = v_ref[:, cv]
        st = st_ref[hd]
        o = _dot(att, v) + _dot_nt((q_dec * jnp.exp(pre)).astype(BF16), st.astype(BF16))
        total = pre + b_last + suf
        st_ref[hd] = st * jnp.exp(total[0:1, :]) + _dot_tn(v, (k_end * jnp.exp(suf)).astype(BF16))
        o = o * lax.rsqrt(jnp.mean(o * o, axis=-1, keepdims=True) + NORM_EPS) * ng_ref[...]
        z = z_ref[:, cv].astype(F32)
        o_ref[:, cv] = (o * _silu(z)).astype(o_ref.dtype)


def _gla(proj, g_lr, wg, bg, ng, bsz, seq):
    tc = GLA_BLOCK
    nt = seq // tc
    row = lambda b, i: b * nt + i
    sums, sel = _gla_constants()
    const = lambda b, i: (0, 0)
    return pl.pallas_call(
        _gla_kernel,
        grid=(bsz, nt),
        in_specs=[pl.BlockSpec((tc, DK_TOT), lambda b, i: (row(b, i), COL_Q // DK_TOT)),
                  pl.BlockSpec((tc, DK_TOT), lambda b, i: (row(b, i), COL_K // DK_TOT)),
                  pl.BlockSpec((tc, LANES), lambda b, i: (row(b, i), 0)),
                  pl.BlockSpec((tc, D_GLA), lambda b, i: (row(b, i), COL_V // D_GLA)),
                  pl.BlockSpec((tc, D_GLA), lambda b, i: (row(b, i), COL_ZG // D_GLA)),
                  pl.BlockSpec((LANES, DK_TOT), const),
                  pl.BlockSpec((1, DK_TOT), const),
                  pl.BlockSpec((1, GLA_DV), const),
                  pl.BlockSpec((GLA_SUMS * tc, tc), const),
                  pl.BlockSpec((tc, tc), const)],
        out_specs=pl.BlockSpec((tc, D_GLA), lambda b, i: (row(b, i), 0)),
        out_shape=jax.ShapeDtypeStruct((bsz * seq, D_GLA), BF16),
        scratch_shapes=[pltpu.VMEM((GLA_HEADS, GLA_DV, GLA_DK), F32)],
        compiler_params=_params("parallel", "arbitrary"),
        name="gla",
    )(proj, proj, g_lr, proj, proj, wg, bg, ng, sums, sel)


def _s5prep_kernel(lre_ref, lim_ref, ldt_ref, bre_ref, bim_ref, are_ref, aim_ref, bbre_ref, bbim_ref):
    lre = lre_ref[...]
    lim = lim_ref[...]
    dt = jnp.exp(ldt_ref[...])
    z_re = lre * dt
    z_im = lim * dt
    mag = jnp.exp(z_re)
    ab_re = mag * jnp.cos(z_im)
    ab_im = mag * jnp.sin(z_im)
    den = lre * lre + lim * lim
    n_re = ab_re - 1.0
    n_im = ab_im
    f_re = (n_re * lre + n_im * lim) / den
    f_im = (n_im * lre - n_re * lim) / den
    b_re = bre_ref[...]
    b_im = bim_ref[...]
    are_ref[...] = ab_re
    aim_ref[...] = ab_im
    bbre_ref[...] = f_re * b_re - f_im * b_im
    bbim_ref[...] = f_re * b_im + f_im * b_re


def _s5prep(lam_re, lam_im, log_dt, bt_re, bt_im):
    g, p = lam_re.shape
    hh = bt_re.shape[1]
    a_shape = jax.ShapeDtypeStruct((g, 1, p), F32)
    b_shape = jax.ShapeDtypeStruct((g, hh, p), F32)
    return pl.pallas_call(
        _s5prep_kernel,
        out_shape=(a_shape, a_shape, b_shape, b_shape),
        name="s5prep",
    )(lam_re.reshape(g, 1, p), lam_im.reshape(g, 1, p), log_dt.reshape(g, 1, 1), bt_re, bt_im)


def _s5_kernel(u0_ref, u1_ref, wb_ref, wc_ref, a_ref, d_ref, o_ref, uf_ref, bu_ref, s_ref, y_ref, carry_ref, *, tc):
    @pl.when(pl.program_id(1) == 0)
    def _():
        carry_ref[...] = jnp.zeros_like(carry_ref)

    nb = u0_ref.shape[0]
    tile = SUBLANES * S5_SUB
    for b in range(nb):
        uf_ref[pl.ds(b, tc, stride=SUBLANES), :] = u0_ref[b].astype(F32)
        uf_ref[pl.ds(nb + b, tc, stride=SUBLANES), :] = u1_ref[b].astype(F32)

    lo = (lax.broadcasted_iota(jnp.int32, (tile, S5_BLOCK_U), 0) & nb) == 0
    a_re = a_ref[0, :, :S5_BLOCK_S]
    a_im = a_ref[0, :, S5_BLOCK_S:]
    s_re = carry_ref[:, :S5_BLOCK_S]
    s_im = carry_ref[:, S5_BLOCK_S:]

    for s in range(tc // S5_SUB):
        base = s * tile
        uf = uf_ref[base:base + tile, :]
        lhs = jnp.concatenate([jnp.where(lo, uf, 0.0), jnp.where(lo, 0.0, uf)], axis=1).astype(BF16)
        bu_ref[base:base + tile, :] = _dot(lhs, wb_ref[0])
        for p in range(S5_SUB // 2):
            r0 = base + p * 2 * SUBLANES
            r1 = r0 + SUBLANES
            m_re = a_re * s_re - a_im * s_im + bu_ref[r0:r1, :S5_BLOCK_S]
            m_im = a_re * s_im + a_im * s_re + bu_ref[r0:r1, S5_BLOCK_S:]
            s_re = a_re * m_re - a_im * m_im + bu_ref[r1:r1 + SUBLANES, :S5_BLOCK_S]
            s_im = a_re * m_im + a_im * m_re + bu_ref[r1:r1 + SUBLANES, S5_BLOCK_S:]
            s_ref[r0:r0 + 2 * SUBLANES, :S5_BLOCK_S] = jnp.concatenate([m_re, s_re], axis=0).astype(BF16)
            s_ref[r0:r0 + 2 * SUBLANES, S5_BLOCK_S:] = jnp.concatenate([m_im, s_im], axis=0).astype(BF16)
        y8 = _dot(s_ref[base:base + tile, :], wc_ref[0])
        y = jnp.where(lo, y8[:, :S5_BLOCK_U], y8[:, S5_BLOCK_U:])
        y = (y.reshape(S5_SUB, SUBLANES, S5_BLOCK_U)
             + d_ref[...] * uf.reshape(S5_SUB, SUBLANES, S5_BLOCK_U)).reshape(tile, S5_BLOCK_U)
        cdf = 0.5 * (1.0 + jnp.tanh(math.sqrt(2.0 / math.pi) * (y + 0.044715 * (y * y * y))))
        y_ref[base:base + tile, :] = y * cdf

    carry_ref[:, :S5_BLOCK_S] = s_re
    carry_ref[:, S5_BLOCK_S:] = s_im
    for j in range(2 * nb):
        o_ref[j] = y_ref[pl.ds(j, tc, stride=SUBLANES), :].astype(o_ref.dtype)


def _s5(proj3, wb, wc, a8, d8):
    bsz, seq, _ = proj3.shape
    nsb = wb.shape[0]
    tc = min(256, seq)
    rows = SUBLANES * tc
    cb = COL_U // S5_BLOCK_U
    return pl.pallas_call(
        functools.partial(_s5_kernel, tc=tc),
        grid=(nsb, seq // tc),
        in_specs=[pl.BlockSpec((bsz, tc, S5_BLOCK_U), lambda sb, i: (0, i, cb + sb)),
                  pl.BlockSpec((bsz, tc, S5_BLOCK_U), lambda sb, i: (0, i, cb + nsb + sb)),
                  pl.BlockSpec((1, 2 * S5_BLOCK_U, 2 * S5_BLOCK_S), lambda sb, i: (sb, 0, 0)),
                  pl.BlockSpec((1, 2 * S5_BLOCK_S, 2 * S5_BLOCK_U), lambda sb, i: (sb, 0, 0)),
                  pl.BlockSpec((1, SUBLANES, 2 * S5_BLOCK_S), lambda sb, i: (sb, 0, 0)),
                  pl.BlockSpec((1, SUBLANES, S5_BLOCK_U), lambda sb, i: (sb, 0, 0))],
        out_specs=pl.BlockSpec((2 * bsz, tc, S5_BLOCK_U), lambda sb, i: (0, i, sb)),
        out_shape=jax.ShapeDtypeStruct((2 * bsz, seq, nsb * S5_BLOCK_U), BF16),
        scratch_shapes=[pltpu.VMEM((rows, S5_BLOCK_U), F32),
                        pltpu.VMEM((rows, 2 * S5_BLOCK_S), F32),
                        pltpu.VMEM((rows, 2 * S5_BLOCK_S), BF16),
                        pltpu.VMEM((rows, S5_BLOCK_U), F32),
                        pltpu.VMEM((SUBLANES, 2 * S5_BLOCK_S), F32)],
        compiler_params=_params("parallel", "arbitrary"),
        name="s5",
    )(proj3, proj3, wb, wc, a8, d8)


def _glu_kernel(y0_ref, y1_ref, z_ref, w_ref, b_ref, o_ref):
    y0 = y0_ref[0]
    y1 = y1_ref[0]
    half = y0.shape[1]
    acc = _dot(y0, w_ref[:half, :]) + _dot(y1, w_ref[half:, :]) + b_ref[...]
    y = jnp.concatenate([y0, y1], axis=1).astype(F32)
    z = z_ref[...].astype(F32)
    o_ref[...] = (y * _sigmoid(acc) * _silu(z)).astype(o_ref.dtype)


def _glu(yv, proj, w, b, bsz, seq):
    d = w.shape[0]
    half = d // 2
    tm = min(512, seq)
    nt = seq // tm
    return pl.pallas_call(
        _glu_kernel,
        grid=(bsz, nt),
        in_specs=[pl.BlockSpec((1, tm, half), lambda b_, i: (b_, i, 0)),
                  pl.BlockSpec((1, tm, half), lambda b_, i: (bsz + b_, i, 0)),
                  pl.BlockSpec((tm, d), lambda b_, i: (b_ * nt + i, COL_ZS // D_S5)),
                  pl.BlockSpec((d, d), lambda b_, i: (0, 0)),
                  pl.BlockSpec((1, d), lambda b_, i: (0, 0))],
        out_specs=pl.BlockSpec((tm, d), lambda b_, i: (b_ * nt + i, 0)),
        out_shape=jax.ShapeDtypeStruct((bsz * seq, d), BF16),
        compiler_params=_params("parallel", "parallel"),
        name="glu",
    )(yv, yv, proj, w, b)


def _out_kernel(og_ref, os_ref, w_ref, x_ref, gate_ref, lg_ref, lb_ref, o_ref, *, nk):
    k = pl.program_id(2)
    half = nk // 2

    @pl.when(k == 0)
    def _():
        o_ref[0] = _dot(og_ref[...], w_ref[...])

    @pl.when((k > 0) & (k < half))
    def _():
        o_ref[0] += _dot(og_ref[...], w_ref[...])

    @pl.when((k >= half) & (k < nk - 1))
    def _():
        o_ref[0] += _dot(os_ref[...], w_ref[...])

    @pl.when(k == nk - 1)
    def _():
        mixed = o_ref[0] + _dot(os_ref[...], w_ref[...])
        r = DEEPNORM_ALPHA * x_ref[0] + gate_ref[0] * mixed
        mu = jnp.mean(r, axis=-1, keepdims=True)
        rc = r - mu
        var = jnp.mean(rc * rc, axis=-1, keepdims=True)
        o_ref[0] = rc * lax.rsqrt(var + NORM_EPS) * lg_ref[...] + lb_ref[...]


def _outproj(og, osb, w, x, mod3, lg, lb):
    bsz, seq, d = x.shape
    dh = og.shape[1]
    tm = min(512, seq)
    nt = seq // tm
    tk = 512
    half = dh // tk
    nk = 2 * half
    return pl.pallas_call(
        functools.partial(_out_kernel, nk=nk),
        grid=(bsz, nt, nk),
        in_specs=[pl.BlockSpec((tm, tk), lambda b, i, k: (b * nt + i, jnp.minimum(k, half - 1))),
                  pl.BlockSpec((tm, tk), lambda b, i, k: (b * nt + i, jnp.maximum(k - half, 0))),
                  pl.BlockSpec((tk, d), lambda b, i, k: (k, 0)),
                  pl.BlockSpec((1, tm, d), lambda b, i, k: (b, i, 0)),
                  pl.BlockSpec((1, 1, d), lambda b, i, k: (b, 0, 2)),
                  pl.BlockSpec((1, d), lambda b, i, k: (0, 0)),
                  pl.BlockSpec((1, d), lambda b, i, k: (0, 0))],
        out_specs=pl.BlockSpec((1, tm, d), lambda b, i, k: (b, i, 0)),
        out_shape=jax.ShapeDtypeStruct(x.shape, x.dtype),
        compiler_params=_params("parallel", "parallel", "arbitrary"),
        name="outproj",
    )(og, osb, w, x, mod3, lg, lb)


def _s5_layouts(ab_re, ab_im, bbt_re, bbt_im, c_re, c_im, d_skip, bsz):
    g = ab_re.shape[0]
    nsb = g // (2 * S5_BLOCK_GROUPS)
    eye = jnp.eye(S5_BLOCK_GROUPS, dtype=F32)

    def in_map(bbt):
        t = bbt.reshape(2, nsb, S5_BLOCK_GROUPS, S5_GROUP, S5_STATE)
        t = jnp.einsum('fsihp,ij->sfihjp', t, eye)
        return t.reshape(nsb, 2 * S5_BLOCK_U, S5_BLOCK_S)

    def out_map(c):
        t = c.reshape(2, nsb, S5_BLOCK_GROUPS, S5_GROUP, S5_STATE)
        t = jnp.einsum('fsjhp,ij->sipfjh', t, eye)
        return t.reshape(nsb, S5_BLOCK_S, 2 * S5_BLOCK_U)

    def rows8(v, width):
        t = v.reshape(2, nsb, 1, width).transpose(1, 0, 2, 3)
        return jnp.broadcast_to(t, (nsb, 2, bsz, width)).reshape(nsb, 2 * bsz, width)

    wb = jnp.concatenate([in_map(bbt_re), in_map(bbt_im)], axis=2).astype(BF16)
    wc = jnp.concatenate([out_map(c_re), -out_map(c_im)], axis=1).astype(BF16)
    a8 = jnp.concatenate([rows8(ab_re.reshape(-1), S5_BLOCK_S), rows8(ab_im.reshape(-1), S5_BLOCK_S)], axis=2)
    d8 = rows8(d_skip, S5_BLOCK_U)
    return wb, wc, a8, d8


def kernel(x, c, w_ada, b_ada, w_in, w_gla_gate, b_gla_gate, gla_norm_g, s5_lambda_re, s5_lambda_im, s5_log_dt, s5_b_re, s5_b_im, s5_c_re, s5_c_im, s5_d, w_glu, b_glu, w_out, ln_g, ln_b):
    bsz, seq, d = x.shape
    assert bsz * 2 == SUBLANES and w_ada.shape[0] == DEPTH and s5_d.shape[1] == D_S5
    layer = 0

    c8 = jnp.pad(c, ((0, SUBLANES - bsz), (0, 0)))
    mod = _ada(c8, w_ada[layer], b_ada[layer][None, :])[:bsz]
    mod3 = mod.reshape(bsz, 1, 3 * d)

    w = w_in[layer].astype(BF16)
    o_g = 2 * DK_TOT + D_GLA
    w_main = jnp.concatenate([w[:, :o_g], w[:, o_g + GLA_GATE_RANK:]], axis=1)
    w_glr = jnp.pad(w[:, o_g:o_g + GLA_GATE_RANK], ((0, 0), (0, LANES - GLA_GATE_RANK)))
    proj, g_lr = _inproj(x, mod3, w_main, w_glr)

    wg = jnp.pad(w_gla_gate[layer], ((0, LANES - GLA_GATE_RANK), (0, 0))).astype(BF16)
    o_gla = _gla(proj, g_lr, wg, b_gla_gate[layer][None, :], gla_norm_g[layer][None, :], bsz, seq)

    bt_re = s5_b_re[layer].transpose(0, 2, 1)
    bt_im = s5_b_im[layer].transpose(0, 2, 1)
    ab_re, ab_im, bbt_re, bbt_im = _s5prep(s5_lambda_re[layer], s5_lambda_im[layer], s5_log_dt[layer],
                                            bt_re, bt_im)
    wb, wc, a8, d8 = _s5_layouts(ab_re, ab_im, bbt_re, bbt_im, s5_c_re[layer], s5_c_im[layer],
                                 s5_d[layer], bsz)
    yv = _s5(proj.reshape(bsz, seq, N_PROJ), wb, wc, a8, d8)
    o_s5 = _glu(yv, proj, w_glu[layer].astype(BF16), b_glu[layer][None, :], bsz, seq)

    return _outproj(o_gla, o_s5, w_out[layer].astype(BF16), x, mod3,
                    ln_g[layer][None, :], ln_b[layer][None, :])
```

```python
import functools
import math

import jax
import jax.numpy as jnp
from jax import lax
from jax.experimental import pallas as pl
from jax.experimental.pallas import tpu as pltpu

F32 = jnp.float32
BF16 = jnp.bfloat16

GLA_HEADS = 4
GLA_DK = 256
GLA_DV = 512
GLA_GATE_RANK = 16
GLA_GATE_TAU = 16.0
GLA_CHUNK = 64
S5_GROUP = 16
S5_STATE = 64
NORM_EPS = 1e-5
DEPTH = 1
DEEPNORM_ALPHA = (2.0 * DEPTH) ** 0.25

LANES = 128
SUBLANES = 8
VMEM_LIMIT = 56 * 1024 * 1024

DK_TOT = GLA_HEADS * GLA_DK
D_GLA = GLA_HEADS * GLA_DV
D_S5 = 2048
COL_Q = 0
COL_K = COL_Q + DK_TOT
COL_V = COL_K + DK_TOT
COL_ZG = COL_V + D_GLA
COL_U = COL_ZG + D_GLA
COL_ZS = COL_U + D_S5
N_PROJ = COL_ZS + D_S5
MXU_WIDTH = 256
INPROJ_TN = 5 * MXU_WIDTH

S5_BLOCK_GROUPS = 8
S5_BLOCK_U = S5_BLOCK_GROUPS * S5_GROUP
S5_BLOCK_S = S5_BLOCK_GROUPS * S5_STATE
S5_SUB = 32


def _sigmoid(v):
    return 1.0 / (1.0 + jnp.exp(-v))


def _silu(v):
    return v * _sigmoid(v)


def _dot(a, b):
    return jnp.dot(a, b, preferred_element_type=F32)


def _dot_nt(a, b):
    return lax.dot_general(a, b, (((1,), (1,)), ((), ())), preferred_element_type=F32)


def _dot_tn(a, b):
    return lax.dot_general(a, b, (((0,), (0,)), ((), ())), preferred_element_type=F32)


def _params(*sem):
    return pltpu.CompilerParams(dimension_semantics=sem, vmem_limit_bytes=VMEM_LIMIT)


def _ada_kernel(c_ref, w_ref, b_ref, o_ref):
    sc = _silu(c_ref[...])
    o_ref[...] = jnp.dot(sc, w_ref[...], preferred_element_type=F32,
                         precision=lax.Precision.HIGHEST) + b_ref[...]


def _ada(c8, w, b):
    d, n = w.shape
    tn = 512
    return pl.pallas_call(
        _ada_kernel,
        grid=(n // tn,),
        in_specs=[pl.BlockSpec((SUBLANES, d), lambda j: (0, 0)),
                  pl.BlockSpec((d, tn), lambda j: (0, j)),
                  pl.BlockSpec((1, tn), lambda j: (0, j))],
        out_specs=pl.BlockSpec((SUBLANES, tn), lambda j: (0, j)),
        out_shape=jax.ShapeDtypeStruct((SUBLANES, n), F32),
        compiler_params=_params("parallel"),
        name="ada",
    )(c8, w, b)


def _wprep_kernel(w_ref, o_ref, g_ref):
    o_g = COL_ZG
    o_ref[:, :o_g] = w_ref[0, :, :o_g].astype(BF16)
    o_ref[:, o_g:] = w_ref[0, :, o_g + GLA_GATE_RANK:].astype(BF16)
    lane = lax.broadcasted_iota(jnp.int32, g_ref.shape, 1)
    g_ref[...] = jnp.where(lane < GLA_GATE_RANK, w_ref[0, :, o_g:o_g + LANES], 0.0).astype(BF16)


def _wprep(w_in):
    _, d, n = w_in.shape
    tr = 256
    return pl.pallas_call(
        _wprep_kernel,
        grid=(d // tr,),
        in_specs=[pl.BlockSpec((1, tr, n), lambda i: (0, i, 0))],
        out_specs=(pl.BlockSpec((tr, N_PROJ), lambda i: (i, 0)),
                   pl.BlockSpec((tr, LANES), lambda i: (i, 0))),
        out_shape=(jax.ShapeDtypeStruct((d, N_PROJ), BF16),
                   jax.ShapeDtypeStruct((d, LANES), BF16)),
        compiler_params=_params("parallel"),
        name="wprep",
    )(w_in)


def _inproj_kernel(x_ref, shift_ref, scale_ref, w_ref, wg_ref, o_ref, g_ref, h_ref):
    @pl.when(pl.program_id(2) == 0)
    def _():
        h_ref[...] = (x_ref[0] * (1.0 + scale_ref[0]) + shift_ref[0]).astype(BF16)
        g_ref[...] = _dot(h_ref[...], wg_ref[...]).astype(g_ref.dtype)

    o_ref[...] = _dot(h_ref[...], w_ref[...]).astype(o_ref.dtype)


def _inproj(x, mod3, w, wg):
    bsz, seq, d = x.shape
    n = w.shape[1]
    tn = INPROJ_TN
    tm = min(512, seq)
    nt = seq // tm
    return pl.pallas_call(
        _inproj_kernel,
        grid=(bsz, nt, n // tn),
        in_specs=[pl.BlockSpec((1, tm, d), lambda b, i, j: (b, i, 0)),
                  pl.BlockSpec((1, 1, d), lambda b, i, j: (b, 0, 0)),
                  pl.BlockSpec((1, 1, d), lambda b, i, j: (b, 0, 1)),
                  pl.BlockSpec((d, tn), lambda b, i, j: (0, j)),
                  pl.BlockSpec((d, LANES), lambda b, i, j: (0, 0))],
        out_specs=(pl.BlockSpec((tm, tn), lambda b, i, j: (b * nt + i, j)),
                   pl.BlockSpec((tm, LANES), lambda b, i, j: (b * nt + i, 0))),
        out_shape=(jax.ShapeDtypeStruct((bsz * seq, n), BF16),
                   jax.ShapeDtypeStruct((bsz * seq, LANES), BF16)),
        scratch_shapes=[pltpu.VMEM((tm, d), BF16)],
        compiler_params=_params("parallel", "parallel", "arbitrary"),
        name="inproj",
    )(x, mod3, mod3, w, wg)


GLA_BLOCK = 4 * GLA_CHUNK
GLA_NCHUNK = GLA_BLOCK // GLA_CHUNK


def _gla_constants():
    import numpy as np
    tc = GLA_BLOCK
    row = np.arange(tc)[:, None]
    col = np.arange(tc)[None, :]
    rc = row // GLA_CHUNK
    cc = col // GLA_CHUNK
    half = GLA_NCHUNK // 2
    same = rc == cc
    totals = np.arange(2 * SUBLANES)[:, None] == cc
    sums = np.concatenate([same & (col <= row), totals], axis=0)
    sel = np.where(same & (col <= row), 1.0,
                   np.where((rc == cc + 1) & (rc != half), 2.0,
                            np.where((rc >= half) & (cc < half), 3.0, 0.0)))
    return jnp.asarray(sums.astype(np.float32), dtype=BF16), jnp.asarray(sel.astype(np.float32))


def _gla_kernel(q_ref, k_ref, g_ref, v_ref, z_ref, wg_ref, bg_ref, ng_ref, sums_ref, sel_ref, o_ref, st_ref):
    @pl.when(pl.program_id(1) == 0)
    def _():
        st_ref[...] = jnp.zeros_like(st_ref)

    tc = GLA_BLOCK
    logit = _dot(g_ref[...], wg_ref[...]) + bg_ref[...]
    la_all = (jnp.minimum(logit, 0.0) - jnp.log(1.0 + jnp.exp(-jnp.abs(logit)))) * (1.0 / GLA_GATE_TAU)
    sel = sel_ref[...]
    for hd in range(GLA_HEADS):
        ck = slice(hd * GLA_DK, (hd + 1) * GLA_DK)
        cv = slice(hd * GLA_DV, (hd + 1) * GLA_DV)
        la = la_all[:, ck]
        la_hi = la.astype(BF16)
        la_lo = (la - la_hi.astype(F32)).astype(BF16)
        sums = _dot(sums_ref[...], la_hi) + _dot(sums_ref[...], la_lo)
        b = sums[:tc]
        tot = [sums[tc + c:tc + c + 1] for c in range(GLA_NCHUNK)]
        half = GLA_NCHUNK // 2

        def span(lo, hi):
            return sum(tot[lo:hi]) if hi > lo else jnp.zeros_like(tot[0])

        k = k_ref[:, ck].astype(F32)
        q_dec = q_ref[:, ck].astype(F32) * (GLA_DK ** -0.5) * jnp.exp(b)
        k_inv = k * jnp.exp(-b)
        k_end, q_in, k_st, q_mid, k_mid = [], [], [], [], []
        for c in range(GLA_NCHUNK):
            rows = slice(c * GLA_CHUNK, (c + 1) * GLA_CHUNK)
            ke = k[rows] * jnp.exp(tot[c] - b[rows])
            e_mid = jnp.exp(span(half, c) if c >= half else span(c + 1, half))
            k_end.append(ke)
            q_in.append(q_dec[rows] * jnp.exp(span(0, c)))
            k_st.append(ke * jnp.exp(span(c + 1, GLA_NCHUNK)))
            q_mid.append(q_dec[rows] * e_mid)
            k_mid.append(ke * e_mid)
        cat = lambda parts: jnp.concatenate(parts, axis=0).astype(BF16)
        a_same = _dot_nt(q_dec.astype(BF16), k_inv.astype(BF16))
        a_next = _dot_nt(q_dec.astype(BF16), cat(k_end))
        a_mid = _dot_nt(cat(q_mid), cat(k_mid))
        att = jnp.where(sel == 1.0, a_same,
                        jnp.where(sel == 2.0, a_next, jnp.where(sel == 3.0, a_mid, 0.0))).astype(BF16)
        v = v_ref[:, cv]
        st = st_ref[hd]
        o = _dot(att, v) + _dot_nt(cat(q_in), st.astype(BF16))
        st_ref[hd] = st * jnp.exp(span(0, GLA_NCHUNK)) + _dot_tn(v, cat(k_st))
        o = o * lax.rsqrt(jnp.mean(o * o, axis=-1, keepdims=True) + NORM_EPS) * ng_ref[...]
        z = z_ref[:, cv].astype(F32)
        o_ref[:, cv] = (o * _silu(z)).astype(o_ref.dtype)


def _gla(proj, g_lr, wg, bg, ng, bsz, seq):
    tc = GLA_BLOCK
    nt = seq // tc
    row = lambda b, i: b * nt + i
    sums, sel = _gla_constants()
    const = lambda b, i: (0, 0)
    return pl.pallas_call(
        _gla_kernel,
        grid=(bsz, nt),
        in_specs=[pl.BlockSpec((tc, DK_TOT), lambda b, i: (row(b, i), COL_Q // DK_TOT)),
                  pl.BlockSpec((tc, DK_TOT), lambda b, i: (row(b, i), COL_K // DK_TOT)),
                  pl.BlockSpec((tc, LANES), lambda b, i: (row(b, i), 0)),
                  pl.BlockSpec((tc, D_GLA), lambda b, i: (row(b, i), COL_V // D_GLA)),
                  pl.BlockSpec((tc, D_GLA), lambda b, i: (row(b, i), COL_ZG // D_GLA)),
                  pl.BlockSpec((LANES, DK_TOT), const),
                  pl.BlockSpec((1, DK_TOT), const),
                  pl.BlockSpec((1, GLA_DV), const),
                  pl.BlockSpec((tc + 2 * SUBLANES, tc), const),
                  pl.BlockSpec((tc, tc), const)],
        out_specs=pl.BlockSpec((tc, D_GLA), lambda b, i: (row(b, i), 0)),
        out_shape=jax.ShapeDtypeStruct((bsz * seq, D_GLA), BF16),
        scratch_shapes=[pltpu.VMEM((GLA_HEADS, GLA_DV, GLA_DK), F32)],
        compiler_params=_params("parallel", "arbitrary"),
        name="gla",
    )(proj, proj, g_lr, proj, proj, wg, bg, ng, sums, sel)


def _s5prep_kernel(lre_ref, lim_ref, ldt_ref, bre_ref, bim_ref, are_ref, aim_ref, bbre_ref, bbim_ref):
    lre = lre_ref[...]
    lim = lim_ref[...]
    dt = jnp.exp(ldt_ref[...])
    z_re = lre * dt
    z_im = lim * dt
    mag = jnp.exp(z_re)
    ab_re = mag * jnp.cos(z_im)
    ab_im = mag * jnp.sin(z_im)
    den = lre * lre + lim * lim
    n_re = ab_re - 1.0
    n_im = ab_im
    f_re = (n_re * lre + n_im * lim) / den
    f_im = (n_im * lre - n_re * lim) / den
    b_re = bre_ref[...]
    b_im = bim_ref[...]
    are_ref[...] = ab_re
    aim_ref[...] = ab_im
    bbre_ref[...] = f_re * b_re - f_im * b_im
    bbim_ref[...] = f_re * b_im + f_im * b_re


def _s5prep(lam_re, lam_im, log_dt, bt_re, bt_im):
    g, p = lam_re.shape
    hh = bt_re.shape[1]
    a_shape = jax.ShapeDtypeStruct((g, 1, p), F32)
    b_shape = jax.ShapeDtypeStruct((g, hh, p), F32)
    return pl.pallas_call(
        _s5prep_kernel,
        out_shape=(a_shape, a_shape, b_shape, b_shape),
        name="s5prep",
    )(lam_re.reshape(g, 1, p), lam_im.reshape(g, 1, p), log_dt.reshape(g, 1, 1), bt_re, bt_im)


def _s5_kernel(u0_ref, u1_ref, wb_ref, wc_ref, a_ref, d_ref, o_ref, uf_ref, bu_ref, s_ref, y_ref, carry_ref, *, tc):
    @pl.when(pl.program_id(1) == 0)
    def _():
        carry_ref[...] = jnp.zeros_like(carry_ref)

    nb = u0_ref.shape[0]
    tile = SUBLANES * S5_SUB
    for b in range(nb):
        uf_ref[pl.ds(b, tc, stride=SUBLANES), :] = u0_ref[b].astype(F32)
        uf_ref[pl.ds(nb + b, tc, stride=SUBLANES), :] = u1_ref[b].astype(F32)

    lo = (lax.broadcasted_iota(jnp.int32, (tile, S5_BLOCK_U), 0) & nb) == 0
    a_re = a_ref[0, :, :S5_BLOCK_S]
    a_im = a_ref[0, :, S5_BLOCK_S:]
    s_re = carry_ref[:, :S5_BLOCK_S]
    s_im = carry_ref[:, S5_BLOCK_S:]

    for s in range(tc // S5_SUB):
        base = s * tile
        uf = uf_ref[base:base + tile, :]
        lhs = jnp.concatenate([jnp.where(lo, uf, 0.0), jnp.where(lo, 0.0, uf)], axis=1).astype(BF16)
        bu_ref[base:base + tile, :] = _dot(lhs, wb_ref[0])
        for p in range(S5_SUB // 2):
            r0 = base + p * 2 * SUBLANES
            r1 = r0 + SUBLANES
            m_re = a_re * s_re - a_im * s_im + bu_ref[r0:r1, :S5_BLOCK_S]
            m_im = a_re * s_im + a_im * s_re + bu_ref[r0:r1, S5_BLOCK_S:]
            s_re = a_re * m_re - a_im * m_im + bu_ref[r1:r1 + SUBLANES, :S5_BLOCK_S]
            s_im = a_re * m_im + a_im * m_re + bu_ref[r1:r1 + SUBLANES, S5_BLOCK_S:]
            s_ref[r0:r0 + 2 * SUBLANES, :S5_BLOCK_S] = jnp.concatenate([m_re, s_re], axis=0).astype(BF16)
            s_ref[r0:r0 + 2 * SUBLANES, S5_BLOCK_S:] = jnp.concatenate([m_im, s_im], axis=0).astype(BF16)
        y8 = _dot(s_ref[base:base + tile, :], wc_ref[0])
        y = jnp.where(lo, y8[:, :S5_BLOCK_U], y8[:, S5_BLOCK_U:])
        y = (y.reshape(S5_SUB, SUBLANES, S5_BLOCK_U)
             + d_ref[...] * uf.reshape(S5_SUB, SUBLANES, S5_BLOCK_U)).reshape(tile, S5_BLOCK_U)
        cdf = 0.5 * (1.0 + jnp.tanh(math.sqrt(2.0 / math.pi) * (y + 0.044715 * (y * y * y))))
        y_ref[base:base + tile, :] = y * cdf

    carry_ref[:, :S5_BLOCK_S] = s_re
    carry_ref[:, S5_BLOCK_S:] = s_im
    for j in range(2 * nb):
        o_ref[j] = y_ref[pl.ds(j, tc, stride=SUBLANES), :].astype(o_ref.dtype)


def _s5(proj3, wb, wc, a8, d8):
    bsz, seq, _ = proj3.shape
    nsb = wb.shape[0]
    tc = min(256, seq)
    rows = SUBLANES * tc
    cb = COL_U // S5_BLOCK_U
    return pl.pallas_call(
        functools.partial(_s5_kernel, tc=tc),
        grid=(nsb, seq // tc),
        in_specs=[pl.BlockSpec((bsz, tc, S5_BLOCK_U), lambda sb, i: (0, i, cb + sb)),
                  pl.BlockSpec((bsz, tc, S5_BLOCK_U), lambda sb, i: (0, i, cb + nsb + sb)),
                  pl.BlockSpec((1, 2 * S5_BLOCK_U, 2 * S5_BLOCK_S), lambda sb, i: (sb, 0, 0)),
                  pl.BlockSpec((1, 2 * S5_BLOCK_S, 2 * S5_BLOCK_U), lambda sb, i: (sb, 0, 0)),
                  pl.BlockSpec((1, SUBLANES, 2 * S5_BLOCK_S), lambda sb, i: (sb, 0, 0)),
                  pl.BlockSpec((1, SUBLANES, S5_BLOCK_U), lambda sb, i: (sb, 0, 0))],
        out_specs=pl.BlockSpec((2 * bsz, tc, S5_BLOCK_U), lambda sb, i: (0, i, sb)),
        out_shape=jax.ShapeDtypeStruct((2 * bsz, seq, nsb * S5_BLOCK_U), BF16),
        scratch_shapes=[pltpu.VMEM((rows, S5_BLOCK_U), F32),
                        pltpu.VMEM((rows, 2 * S5_BLOCK_S), F32),
                        pltpu.VMEM((rows, 2 * S5_BLOCK_S), BF16),
                        pltpu.VMEM((rows, S5_BLOCK_U), F32),
                        pltpu.VMEM((SUBLANES, 2 * S5_BLOCK_S), F32)],
        compiler_params=_params("parallel", "arbitrary"),
        name="s5",
    )(proj3, proj3, wb, wc, a8, d8)


def _glu_kernel(y0_ref, y1_ref, z_ref, w_ref, b_ref, o_ref):
    y0 = y0_ref[0]
    y1 = y1_ref[0]
    half = y0.shape[1]
    acc = _dot(y0, w_ref[:half, :]) + _dot(y1, w_ref[half:, :]) + b_ref[...]
    y = jnp.concatenate([y0, y1], axis=1).astype(F32)
    z = z_ref[...].astype(F32)
    o_ref[...] = (y * _sigmoid(acc) * _silu(z)).astype(o_ref.dtype)


def _glu(yv, proj, w, b, bsz, seq):
    d = w.shape[0]
    half = d // 2
    tm = min(512, seq)
    nt = seq // tm
    return pl.pallas_call(
        _glu_kernel,
        grid=(bsz, nt),
        in_specs=[pl.BlockSpec((1, tm, half), lambda b_, i: (b_, i, 0)),
                  pl.BlockSpec((1, tm, half), lambda b_, i: (bsz + b_, i, 0)),
                  pl.BlockSpec((tm, d), lambda b_, i: (b_ * nt + i, COL_ZS // D_S5)),
                  pl.BlockSpec((d, d), lambda b_, i: (0, 0)),
                  pl.BlockSpec((1, d), lambda b_, i: (0, 0))],
        out_specs=pl.BlockSpec((tm, d), lambda b_, i: (b_ * nt + i, 0)),
        out_shape=jax.ShapeDtypeStruct((bsz * seq, d), BF16),
        compiler_params=_params("parallel", "parallel"),
        name="glu",
    )(yv, yv, proj, w, b)


def _out_kernel(og_ref, os_ref, w_ref, x_hbm, gate_ref, lg_ref, lb_ref, o_ref, x_buf, x_sem, *, nk, tm):
    b = pl.program_id(0)
    i = pl.program_id(1)
    k = pl.program_id(2)
    half = nk // 2

    def x_copy():
        r0 = pl.multiple_of(i * tm, tm)
        return pltpu.make_async_copy(x_hbm.at[b, pl.ds(r0, tm), :], x_buf, x_sem)

    @pl.when(k == 0)
    def _():
        x_copy().start()
        o_ref[0] = _dot(og_ref[...], w_ref[...])

    @pl.when((k > 0) & (k < half))
    def _():
        o_ref[0] += _dot(og_ref[...], w_ref[...])

    @pl.when((k >= half) & (k < nk - 1))
    def _():
        o_ref[0] += _dot(os_ref[...], w_ref[...])

    @pl.when(k == nk - 1)
    def _():
        mixed = o_ref[0] + _dot(os_ref[...], w_ref[...])
        x_copy().wait()
        r = DEEPNORM_ALPHA * x_buf[...] + gate_ref[0] * mixed
        mu = jnp.mean(r, axis=-1, keepdims=True)
        rc = r - mu
        var = jnp.mean(rc * rc, axis=-1, keepdims=True)
        o_ref[0] = rc * lax.rsqrt(var + NORM_EPS) * lg_ref[...] + lb_ref[...]


def _outproj(og, osb, w, x, mod3, lg, lb):
    bsz, seq, d = x.shape
    dh = og.shape[1]
    tm = min(512, seq)
    nt = seq // tm
    tk = 1024
    half = dh // tk
    nk = 2 * half
    return pl.pallas_call(
        functools.partial(_out_kernel, nk=nk, tm=tm),
        grid=(bsz, nt, nk),
        in_specs=[pl.BlockSpec((tm, tk), lambda b, i, k: (b * nt + i, jnp.minimum(k, half - 1))),
                  pl.BlockSpec((tm, tk), lambda b, i, k: (b * nt + i, jnp.maximum(k - half, 0))),
                  pl.BlockSpec((tk, d), lambda b, i, k: (k, 0)),
                  pl.BlockSpec(memory_space=pl.ANY),
                  pl.BlockSpec((1, 1, d), lambda b, i, k: (b, 0, 2)),
                  pl.BlockSpec((1, d), lambda b, i, k: (0, 0)),
                  pl.BlockSpec((1, d), lambda b, i, k: (0, 0))],
        out_specs=pl.BlockSpec((1, tm, d), lambda b, i, k: (b, i, 0)),
        out_shape=jax.ShapeDtypeStruct(x.shape, x.dtype),
        scratch_shapes=[pltpu.VMEM((tm, d), F32), pltpu.SemaphoreType.DMA(())],
        compiler_params=_params("parallel", "parallel", "arbitrary"),
        name="outproj",
    )(og, osb, w, x, mod3, lg, lb)


def _s5_layouts(ab_re, ab_im, bbt_re, bbt_im, c_re, c_im, d_skip, bsz):
    g = ab_re.shape[0]
    nsb = g // (2 * S5_BLOCK_GROUPS)
    eye = jnp.eye(S5_BLOCK_GROUPS, dtype=F32)

    def in_map(bbt):
        t = bbt.reshape(2, nsb, S5_BLOCK_GROUPS, S5_GROUP, S5_STATE)
        t = jnp.einsum('fsihp,ij->sfihjp', t, eye)
        return t.reshape(nsb, 2 * S5_BLOCK_U, S5_BLOCK_S)

    def out_map(c):
        t = c.reshape(2, nsb, S5_BLOCK_GROUPS, S5_GROUP, S5_STATE)
        t = jnp.einsum('fsjhp,ij->sipfjh', t, eye)
        return t.reshape(nsb, S5_BLOCK_S, 2 * S5_BLOCK_U)

    def rows8(v, width):
        t = v.reshape(2, nsb, 1, width).transpose(1, 0, 2, 3)
        return jnp.broadcast_to(t, (nsb, 2, bsz, width)).reshape(nsb, 2 * bsz, width)

    wb = jnp.concatenate([in_map(bbt_re), in_map(bbt_im)], axis=2).astype(BF16)
    wc = jnp.concatenate([out_map(c_re), -out_map(c_im)], axis=1).astype(BF16)
    a8 = jnp.concatenate([rows8(ab_re.reshape(-1), S5_BLOCK_S), rows8(ab_im.reshape(-1), S5_BLOCK_S)], axis=2)
    d8 = rows8(d_skip, S5_BLOCK_U)
    return wb, wc, a8, d8


def kernel(x, c, w_ada, b_ada, w_in, w_gla_gate, b_gla_gate, gla_norm_g, s5_lambda_re, s5_lambda_im, s5_log_dt, s5_b_re, s5_b_im, s5_c_re, s5_c_im, s5_d, w_glu, b_glu, w_out, ln_g, ln_b):
    bsz, seq, d = x.shape
    assert bsz * 2 == SUBLANES and w_ada.shape[0] == DEPTH and s5_d.shape[1] == D_S5
    layer = 0

    c8 = jnp.pad(c, ((0, SUBLANES - bsz), (0, 0)))
    mod = _ada(c8, w_ada[layer], b_ada[layer][None, :])[:bsz]
    mod3 = mod.reshape(bsz, 1, 3 * d)

    w_main, w_glr = _wprep(w_in[layer:layer + 1])
    proj, g_lr = _inproj(x, mod3, w_main, w_glr)

    wg = jnp.pad(w_gla_gate[layer], ((0, LANES - GLA_GATE_RANK), (0, 0))).astype(BF16)
    o_gla = _gla(proj, g_lr, wg, b_gla_gate[layer][None, :], gla_norm_g[layer][None, :], bsz, seq)

    bt_re = s5_b_re[layer].transpose(0, 2, 1)
    bt_im = s5_b_im[layer].transpose(0, 2, 1)
    ab_re, ab_im, bbt_re, bbt_im = _s5prep(s5_lambda_re[layer], s5_lambda_im[layer], s5_log_dt[layer],
                                            bt_re, bt_im)
    wb, wc, a8, d8 = _s5_layouts(ab_re, ab_im, bbt_re, bbt_im, s5_c_re[layer], s5_c_im[layer],
                                 s5_d[layer], bsz)
    yv = _s5(proj.reshape(bsz, seq, N_PROJ), wb, wc, a8, d8)
    o_s5 = _glu(yv, proj, w_glu[layer].astype(BF16), b_glu[layer][None, :], bsz, seq)

    return _outproj(o_gla, o_s5, w_out[layer].astype(BF16), x, mod3,
                    ln_g[layer][None, :], ln_b[layer][None, :])
```

```python
import functools
import math

import jax
import jax.numpy as jnp
from jax import lax
from jax.experimental import pallas as pl
from jax.experimental.pallas import tpu as pltpu

F32 = jnp.float32
BF16 = jnp.bfloat16

GLA_HEADS = 4
GLA_DK = 256
GLA_DV = 512
GLA_GATE_RANK = 16
GLA_GATE_TAU = 16.0
GLA_CHUNK = 64
S5_GROUP = 16
S5_STATE = 64
NORM_EPS = 1e-5
DEPTH = 1
DEEPNORM_ALPHA = (2.0 * DEPTH) ** 0.25

LANES = 128
SUBLANES = 8
VMEM_LIMIT = 56 * 1024 * 1024

DK_TOT = GLA_HEADS * GLA_DK
D_GLA = GLA_HEADS * GLA_DV
D_S5 = 2048
COL_Q = 0
COL_K = COL_Q + DK_TOT
COL_V = COL_K + DK_TOT
COL_ZG = COL_V + D_GLA
COL_U = COL_ZG + D_GLA
COL_ZS = COL_U + D_S5
N_PROJ = COL_ZS + D_S5
MXU_WIDTH = 256
INPROJ_TN = 5 * MXU_WIDTH

S5_BLOCK_GROUPS = 8
S5_BLOCK_U = S5_BLOCK_GROUPS * S5_GROUP
S5_BLOCK_S = S5_BLOCK_GROUPS * S5_STATE
S5_SUB = 32


def _sigmoid(v):
    return 1.0 / (1.0 + jnp.exp(-v))


def _silu(v):
    return v * _sigmoid(v)


def _dot(a, b):
    return jnp.dot(a, b, preferred_element_type=F32)


def _dot_nt(a, b):
    return lax.dot_general(a, b, (((1,), (1,)), ((), ())), preferred_element_type=F32)


def _dot_tn(a, b):
    return lax.dot_general(a, b, (((0,), (0,)), ((), ())), preferred_element_type=F32)


def _params(*sem):
    return pltpu.CompilerParams(dimension_semantics=sem, vmem_limit_bytes=VMEM_LIMIT)


def _ada_kernel(ct_ref, w_ref, b_ref, o_ref, sc_ref):
    nb = sc_ref.shape[0]
    d, tn = w_ref.shape

    @pl.when(pl.program_id(0) == 0)
    def _():
        sc = _silu(ct_ref[...])
        for b in range(nb):
            sc_ref[b] = jnp.broadcast_to(sc[:, b:b + 1], (d, LANES))

    def body(g, accs):
        r0 = pl.multiple_of(g * SUBLANES, SUBLANES)
        w = w_ref[pl.ds(r0, SUBLANES), :]
        return tuple(acc + w * jnp.concatenate([sc_ref[b, pl.ds(r0, SUBLANES), :]] * (tn // LANES), axis=1)
                     for b, acc in enumerate(accs))

    accs = lax.fori_loop(0, d // SUBLANES, body,
                         tuple(jnp.zeros((SUBLANES, tn), F32) for _ in range(nb)), unroll=8)
    for b in range(nb):
        o_ref[b:b + 1, :] = jnp.sum(accs[b], axis=0, keepdims=True) + b_ref[...]


def _ada(ct, w, b):
    d, n = w.shape
    nb = ct.shape[1]
    tn = 512
    return pl.pallas_call(
        _ada_kernel,
        grid=(n // tn,),
        in_specs=[pl.BlockSpec((d, nb), lambda j: (0, 0)),
                  pl.BlockSpec((d, tn), lambda j: (0, j)),
                  pl.BlockSpec((1, tn), lambda j: (0, j))],
        out_specs=pl.BlockSpec((nb, tn), lambda j: (0, j)),
        out_shape=jax.ShapeDtypeStruct((nb, n), F32),
        scratch_shapes=[pltpu.VMEM((nb, d, LANES), F32)],
        compiler_params=_params("arbitrary"),
        name="ada",
    )(ct, w, b)


WPREP_ROWS = 512


def _wprep_kernel(cur_ref, nxt_ref, o_ref, g_ref, *, n_plain):
    i = pl.program_id(0)
    keep = WPREP_ROWS - GLA_GATE_RANK

    @pl.when(i < n_plain)
    def _():
        o_ref[...] = cur_ref[...].astype(BF16)

    @pl.when(i >= n_plain)
    def _():
        o_ref[:keep] = cur_ref[GLA_GATE_RANK:].astype(BF16)
        o_ref[keep:] = nxt_ref[...].astype(BF16)

    @pl.when(i == n_plain - 1)
    def _():
        g_ref[:GLA_GATE_RANK] = nxt_ref[...].astype(BF16)
        g_ref[GLA_GATE_RANK:] = jnp.zeros((LANES - GLA_GATE_RANK, g_ref.shape[1]), BF16)


def _wprep(wt):
    n, d = wt.shape
    assert n == N_PROJ + GLA_GATE_RANK and COL_ZG % WPREP_ROWS == 0 and N_PROJ % WPREP_ROWS == 0
    per = WPREP_ROWS // GLA_GATE_RANK
    return pl.pallas_call(
        functools.partial(_wprep_kernel, n_plain=COL_ZG // WPREP_ROWS),
        grid=(N_PROJ // WPREP_ROWS,),
        in_specs=[pl.BlockSpec((WPREP_ROWS, d), lambda i: (i, 0)),
                  pl.BlockSpec((GLA_GATE_RANK, d), lambda i: ((i + 1) * per, 0))],
        out_specs=(pl.BlockSpec((WPREP_ROWS, d), lambda i: (i, 0)),
                   pl.BlockSpec((LANES, d), lambda i: (0, 0))),
        out_shape=(jax.ShapeDtypeStruct((N_PROJ, d), BF16),
                   jax.ShapeDtypeStruct((LANES, d), BF16)),
        compiler_params=_params("arbitrary"),
        name="wprep",
    )(wt, wt)


def _inproj_kernel(x_ref, shift_ref, scale_ref, w_ref, wg_ref, o_ref, g_ref, h_ref):
    @pl.when(pl.program_id(2) == 0)
    def _():
        h_ref[...] = (x_ref[0] * (1.0 + scale_ref[0]) + shift_ref[0]).astype(BF16)
        g_ref[...] = _dot_nt(h_ref[...], wg_ref[...]).astype(g_ref.dtype)

    o_ref[...] = _dot_nt(h_ref[...], w_ref[...]).astype(o_ref.dtype)


def _inproj(x, mod3, w, wg):
    bsz, seq, d = x.shape
    n = w.shape[0]
    tn = INPROJ_TN
    tm = min(512, seq)
    nt = seq // tm
    return pl.pallas_call(
        _inproj_kernel,
        grid=(bsz, nt, n // tn),
        in_specs=[pl.BlockSpec((1, tm, d), lambda b, i, j: (b, i, 0)),
                  pl.BlockSpec((1, 1, d), lambda b, i, j: (b, 0, 0)),
                  pl.BlockSpec((1, 1, d), lambda b, i, j: (b, 0, 1)),
                  pl.BlockSpec((tn, d), lambda b, i, j: (j, 0)),
                  pl.BlockSpec((LANES, d), lambda b, i, j: (0, 0))],
        out_specs=(pl.BlockSpec((tm, tn), lambda b, i, j: (b * nt + i, j)),
                   pl.BlockSpec((tm, LANES), lambda b, i, j: (b * nt + i, 0))),
        out_shape=(jax.ShapeDtypeStruct((bsz * seq, n), BF16),
                   jax.ShapeDtypeStruct((bsz * seq, LANES), BF16)),
        scratch_shapes=[pltpu.VMEM((tm, d), BF16)],
        compiler_params=_params("parallel", "parallel", "arbitrary"),
        name="inproj",
    )(x, mod3, mod3, w, wg)


GLA_BLOCK = 4 * GLA_CHUNK
GLA_NCHUNK = GLA_BLOCK // GLA_CHUNK


def _gla_constants():
    import numpy as np
    tc = GLA_BLOCK
    row = np.arange(tc)[:, None]
    col = np.arange(tc)[None, :]
    rc = row // GLA_CHUNK
    cc = col // GLA_CHUNK
    half = GLA_NCHUNK // 2
    same = rc == cc
    totals = np.arange(2 * SUBLANES)[:, None] == cc
    sums = np.concatenate([same & (col <= row), totals], axis=0)
    sel = np.where(same & (col <= row), 1.0,
                   np.where((rc == cc + 1) & (rc != half), 2.0,
                            np.where((rc >= half) & (cc < half), 3.0, 0.0)))
    return jnp.asarray(sums.astype(np.float32), dtype=BF16), jnp.asarray(sel.astype(np.float32))


def _gla_kernel(q_ref, k_ref, g_ref, v_ref, z_ref, wg_ref, bg_ref, ng_ref, sums_ref, sel_ref, o_ref, st_ref):
    @pl.when(pl.program_id(1) == 0)
    def _():
        st_ref[...] = jnp.zeros_like(st_ref)

    tc = GLA_BLOCK
    logit = _dot(g_ref[...], wg_ref[...]) + bg_ref[...]
    la_all = (jnp.minimum(logit, 0.0) - jnp.log(1.0 + jnp.exp(-jnp.abs(logit)))) * (1.0 / GLA_GATE_TAU)
    sel = sel_ref[...]
    for hd in range(GLA_HEADS):
        ck = slice(hd * GLA_DK, (hd + 1) * GLA_DK)
        cv = slice(hd * GLA_DV, (hd + 1) * GLA_DV)
        la = la_all[:, ck]
        la_hi = la.astype(BF16)
        la_lo = (la - la_hi.astype(F32)).astype(BF16)
        sums = _dot(sums_ref[...], la_hi) + _dot(sums_ref[...], la_lo)
        b = sums[:tc]
        tot = [sums[tc + c:tc + c + 1] for c in range(GLA_NCHUNK)]
        half = GLA_NCHUNK // 2

        def span(lo, hi):
            return sum(tot[lo:hi]) if hi > lo else jnp.zeros_like(tot[0])

        k = k_ref[:, ck].astype(F32)
        q_dec = q_ref[:, ck].astype(F32) * (GLA_DK ** -0.5) * jnp.exp(b)
        k_inv = k * jnp.exp(-b)
        k_end, q_in, k_st, q_mid, k_mid = [], [], [], [], []
        for c in range(GLA_NCHUNK):
            rows = slice(c * GLA_CHUNK, (c + 1) * GLA_CHUNK)
            ke = k[rows] * jnp.exp(tot[c] - b[rows])
            e_mid = jnp.exp(span(half, c) if c >= half else span(c + 1, half))
            k_end.append(ke)
            q_in.append(q_dec[rows] * jnp.exp(span(0, c)))
            k_st.append(ke * jnp.exp(span(c + 1, GLA_NCHUNK)))
            q_mid.append(q_dec[rows] * e_mid)
            k_mid.append(ke * e_mid)
        cat = lambda parts: jnp.concatenate(parts, axis=0).astype(BF16)
        a_same = _dot_nt(q_dec.astype(BF16), k_inv.astype(BF16))
        a_next = _dot_nt(q_dec.astype(BF16), cat(k_end))
        a_mid = _dot_nt(cat(q_mid), cat(k_mid))
        att = jnp.where(sel == 1.0, a_same,
                        jnp.where(sel == 2.0, a_next, jnp.where(sel == 3.0, a_mid, 0.0))).astype(BF16)
        v = v_ref[:, cv]
        st = st_ref[hd]
        o = _dot(att, v) + _dot_nt(cat(q_in), st.astype(BF16))
        st_ref[hd] = st * jnp.exp(span(0, GLA_NCHUNK)) + _dot_tn(v, cat(k_st))
        o = o * lax.rsqrt(jnp.mean(o * o, axis=-1, keepdims=True) + NORM_EPS) * ng_ref[...]
        z = z_ref[:, cv].astype(F32)
        o_ref[:, cv] = (o * _silu(z)).astype(o_ref.dtype)


def _gla(proj, g_lr, wg, bg, ng, bsz, seq):
    tc = GLA_BLOCK
    nt = seq // tc
    row = lambda b, i: b * nt + i
    sums, sel = _gla_constants()
    const = lambda b, i: (0, 0)
    return pl.pallas_call(
        _gla_kernel,
        grid=(bsz, nt),
        in_specs=[pl.BlockSpec((tc, DK_TOT), lambda b, i: (row(b, i), COL_Q // DK_TOT)),
                  pl.BlockSpec((tc, DK_TOT), lambda b, i: (row(b, i), COL_K // DK_TOT)),
                  pl.BlockSpec((tc, LANES), lambda b, i: (row(b, i), 0)),
                  pl.BlockSpec((tc, D_GLA), lambda b, i: (row(b, i), COL_V // D_GLA)),
                  pl.BlockSpec((tc, D_GLA), lambda b, i: (row(b, i), COL_ZG // D_GLA)),
                  pl.BlockSpec((LANES, DK_TOT), const),
                  pl.BlockSpec((1, DK_TOT), const),
                  pl.BlockSpec((1, GLA_DV), const),
                  pl.BlockSpec((tc + 2 * SUBLANES, tc), const),
                  pl.BlockSpec((tc, tc), const)],
        out_specs=pl.BlockSpec((tc, D_GLA), lambda b, i: (row(b, i), 0)),
        out_shape=jax.ShapeDtypeStruct((bsz * seq, D_GLA), BF16),
        scratch_shapes=[pltpu.VMEM((GLA_HEADS, GLA_DV, GLA_DK), F32)],
        compiler_params=_params("parallel", "arbitrary"),
        name="gla",
    )(proj, proj, g_lr, proj, proj, wg, bg, ng, sums, sel)


def _s5prep_kernel(lre_ref, lim_ref, ldt_ref, bre_ref, bim_ref, are_ref, aim_ref, bbre_ref, bbim_ref):
    lre = lre_ref[...]
    lim = lim_ref[...]
    dt = jnp.exp(ldt_ref[...])
    z_re = lre * dt
    z_im = lim * dt
    mag = jnp.exp(z_re)
    ab_re = mag * jnp.cos(z_im)
    ab_im = mag * jnp.sin(z_im)
    den = lre * lre + lim * lim
    n_re = ab_re - 1.0
    n_im = ab_im
    f_re = (n_re * lre + n_im * lim) / den
    f_im = (n_im * lre - n_re * lim) / den
    b_re = bre_ref[...]
    b_im = bim_ref[...]
    are_ref[...] = ab_re
    aim_ref[...] = ab_im
    bbre_ref[...] = f_re * b_re - f_im * b_im
    bbim_ref[...] = f_re * b_im + f_im * b_re


def _s5prep(lam_re, lam_im, log_dt, bt_re, bt_im):
    g, p = lam_re.shape
    hh = bt_re.shape[1]
    a_shape = jax.ShapeDtypeStruct((g, 1, p), F32)
    b_shape = jax.ShapeDtypeStruct((g, hh, p), F32)
    return pl.pallas_call(
        _s5prep_kernel,
        out_shape=(a_shape, a_shape, b_shape, b_shape),
        name="s5prep",
    )(lam_re.reshape(g, 1, p), lam_im.reshape(g, 1, p), log_dt.reshape(g, 1, 1), bt_re, bt_im)


def _s5_kernel(u0_ref, u1_ref, wb_ref, wc_ref, a_ref, d_ref, o_ref, uf_ref, bu_ref, s_ref, y_ref, carry_ref, *, tc):
    @pl.when(pl.program_id(1) == 0)
    def _():
        carry_ref[...] = jnp.zeros_like(carry_ref)

    nb = u0_ref.shape[0]
    tile = SUBLANES * S5_SUB
    for b in range(nb):
        uf_ref[pl.ds(b, tc, stride=SUBLANES), :] = u0_ref[b].astype(F32)
        uf_ref[pl.ds(nb + b, tc, stride=SUBLANES), :] = u1_ref[b].astype(F32)

    lo = (lax.broadcasted_iota(jnp.int32, (tile, S5_BLOCK_U), 0) & nb) == 0
    a_re = a_ref[0, :, :S5_BLOCK_S]
    a_im = a_ref[0, :, S5_BLOCK_S:]
    s_re = carry_ref[:, :S5_BLOCK_S]
    s_im = carry_ref[:, S5_BLOCK_S:]

    for s in range(tc // S5_SUB):
        base = s * tile
        uf = uf_ref[base:base + tile, :]
        lhs = jnp.concatenate([jnp.where(lo, uf, 0.0), jnp.where(lo, 0.0, uf)], axis=1).astype(BF16)
        bu_ref[base:base + tile, :] = _dot(lhs, wb_ref[0])
        for p in range(S5_SUB // 2):
            r0 = base + p * 2 * SUBLANES
            r1 = r0 + SUBLANES
            m_re = a_re * s_re - a_im * s_im + bu_ref[r0:r1, :S5_BLOCK_S]
            m_im = a_re * s_im + a_im * s_re + bu_ref[r0:r1, S5_BLOCK_S:]
            s_re = a_re * m_re - a_im * m_im + bu_ref[r1:r1 + SUBLANES, :S5_BLOCK_S]
            s_im = a_re * m_im + a_im * m_re + bu_ref[r1:r1 + SUBLANES, S5_BLOCK_S:]
            s_ref[r0:r0 + 2 * SUBLANES, :S5_BLOCK_S] = jnp.concatenate([m_re, s_re], axis=0).astype(BF16)
            s_ref[r0:r0 + 2 * SUBLANES, S5_BLOCK_S:] = jnp.concatenate([m_im, s_im], axis=0).astype(BF16)
        y8 = _dot(s_ref[base:base + tile, :], wc_ref[0])
        y = jnp.where(lo, y8[:, :S5_BLOCK_U], y8[:, S5_BLOCK_U:])
        y = (y.reshape(S5_SUB, SUBLANES, S5_BLOCK_U)
             + d_ref[...] * uf.reshape(S5_SUB, SUBLANES, S5_BLOCK_U)).reshape(tile, S5_BLOCK_U)
        cdf = 0.5 * (1.0 + jnp.tanh(math.sqrt(2.0 / math.pi) * (y + 0.044715 * (y * y * y))))
        y_ref[base:base + tile, :] = y * cdf

    carry_ref[:, :S5_BLOCK_S] = s_re
    carry_ref[:, S5_BLOCK_S:] = s_im
    for j in range(2 * nb):
        o_ref[j] = y_ref[pl.ds(j, tc, stride=SUBLANES), :].astype(o_ref.dtype)


def _s5(proj3, wb, wc, a8, d8):
    bsz, seq, _ = proj3.shape
    nsb = wb.shape[0]
    tc = min(256, seq)
    rows = SUBLANES * tc
    cb = COL_U // S5_BLOCK_U
    return pl.pallas_call(
        functools.partial(_s5_kernel, tc=tc),
        grid=(nsb, seq // tc),
        in_specs=[pl.BlockSpec((bsz, tc, S5_BLOCK_U), lambda sb, i: (0, i, cb + sb)),
                  pl.BlockSpec((bsz, tc, S5_BLOCK_U), lambda sb, i: (0, i, cb + nsb + sb)),
                  pl.BlockSpec((1, 2 * S5_BLOCK_U, 2 * S5_BLOCK_S), lambda sb, i: (sb, 0, 0)),
                  pl.BlockSpec((1, 2 * S5_BLOCK_S, 2 * S5_BLOCK_U), lambda sb, i: (sb, 0, 0)),
                  pl.BlockSpec((1, SUBLANES, 2 * S5_BLOCK_S), lambda sb, i: (sb, 0, 0)),
                  pl.BlockSpec((1, SUBLANES, S5_BLOCK_U), lambda sb, i: (sb, 0, 0))],
        out_specs=pl.BlockSpec((2 * bsz, tc, S5_BLOCK_U), lambda sb, i: (0, i, sb)),
        out_shape=jax.ShapeDtypeStruct((2 * bsz, seq, nsb * S5_BLOCK_U), BF16),
        scratch_shapes=[pltpu.VMEM((rows, S5_BLOCK_U), F32),
                        pltpu.VMEM((rows, 2 * S5_BLOCK_S), F32),
                        pltpu.VMEM((rows, 2 * S5_BLOCK_S), BF16),
                        pltpu.VMEM((rows, S5_BLOCK_U), F32),
                        pltpu.VMEM((SUBLANES, 2 * S5_BLOCK_S), F32)],
        compiler_params=_params("parallel", "arbitrary"),
        name="s5",
    )(proj3, proj3, wb, wc, a8, d8)


def _glu_kernel(y0_ref, y1_ref, z_ref, w_ref, b_ref, o_ref):
    y0 = y0_ref[0]
    y1 = y1_ref[0]
    half = y0.shape[1]
    tn = 2 * MXU_WIDTH
    for j in range(2 * half // tn):
        cs = slice(j * tn, (j + 1) * tn)
        acc = _dot(y0, w_ref[:half, cs]) + _dot(y1, w_ref[half:, cs]) + b_ref[:, cs]
        src = y0 if j * tn < half else y1
        y = src[:, (j * tn) % half:(j * tn) % half + tn].astype(F32)
        z = z_ref[:, cs].astype(F32)
        o_ref[:, cs] = (y * _sigmoid(acc) * _silu(z)).astype(o_ref.dtype)


def _glu(yv, proj, w, b, bsz, seq):
    d = w.shape[0]
    half = d // 2
    tm = min(512, seq)
    nt = seq // tm
    return pl.pallas_call(
        _glu_kernel,
        grid=(bsz, nt),
        in_specs=[pl.BlockSpec((1, tm, half), lambda b_, i: (b_, i, 0)),
                  pl.BlockSpec((1, tm, half), lambda b_, i: (bsz + b_, i, 0)),
                  pl.BlockSpec((tm, d), lambda b_, i: (b_ * nt + i, COL_ZS // D_S5)),
                  pl.BlockSpec((d, d), lambda b_, i: (0, 0)),
                  pl.BlockSpec((1, d), lambda b_, i: (0, 0))],
        out_specs=pl.BlockSpec((tm, d), lambda b_, i: (b_ * nt + i, 0)),
        out_shape=jax.ShapeDtypeStruct((bsz * seq, d), BF16),
        compiler_params=_params("parallel", "parallel"),
        name="glu",
    )(yv, yv, proj, w, b)


def _out_kernel(og_ref, os_ref, w_ref, x_hbm, gate_ref, lg_ref, lb_ref, o_ref, x_buf, x_sem, *, nk, tm):
    b = pl.program_id(0)
    i = pl.program_id(1)
    k = pl.program_id(2)
    half = nk // 2

    def x_copy():
        r0 = pl.multiple_of(i * tm, tm)
        return pltpu.make_async_copy(x_hbm.at[b, pl.ds(r0, tm), :], x_buf, x_sem)

    @pl.when(k == 0)
    def _():
        x_copy().start()
        o_ref[0] = _dot(og_ref[...], w_ref[...])

    @pl.when((k > 0) & (k < half))
    def _():
        o_ref[0] += _dot(og_ref[...], w_ref[...])

    @pl.when((k >= half) & (k < nk - 1))
    def _():
        o_ref[0] += _dot(os_ref[...], w_ref[...])

    @pl.when(k == nk - 1)
    def _():
        mixed = o_ref[0] + _dot(os_ref[...], w_ref[...])
        x_copy().wait()
        r = DEEPNORM_ALPHA * x_buf[...] + gate_ref[0] * mixed
        mu = jnp.mean(r, axis=-1, keepdims=True)
        rc = r - mu
        var = jnp.mean(rc * rc, axis=-1, keepdims=True)
        o_ref[0] = rc * lax.rsqrt(var + NORM_EPS) * lg_ref[...] + lb_ref[...]


def _outproj(og, osb, w, x, mod3, lg, lb):
    bsz, seq, d = x.shape
    dh = og.shape[1]
    tm = min(512, seq)
    nt = seq // tm
    tk = 1024
    half = dh // tk
    nk = 2 * half
    return pl.pallas_call(
        functools.partial(_out_kernel, nk=nk, tm=tm),
        grid=(bsz, nt, nk),
        in_specs=[pl.BlockSpec((tm, tk), lambda b, i, k: (b * nt + i, jnp.minimum(k, half - 1))),
                  pl.BlockSpec((tm, tk), lambda b, i, k: (b * nt + i, jnp.maximum(k - half, 0))),
                  pl.BlockSpec((tk, d), lambda b, i, k: (k, 0)),
                  pl.BlockSpec(memory_space=pl.ANY),
                  pl.BlockSpec((1, 1, d), lambda b, i, k: (b, 0, 2)),
                  pl.BlockSpec((1, d), lambda b, i, k: (0, 0)),
                  pl.BlockSpec((1, d), lambda b, i, k: (0, 0))],
        out_specs=pl.BlockSpec((1, tm, d), lambda b, i, k: (b, i, 0)),
        out_shape=jax.ShapeDtypeStruct(x.shape, x.dtype),
        scratch_shapes=[pltpu.VMEM((tm, d), F32), pltpu.SemaphoreType.DMA(())],
        compiler_params=_params("parallel", "parallel", "arbitrary"),
        name="outproj",
    )(og, osb, w, x, mod3, lg, lb)


def _s5_layouts(ab_re, ab_im, bbt_re, bbt_im, c_re, c_im, d_skip, bsz):
    g = ab_re.shape[0]
    nsb = g // (2 * S5_BLOCK_GROUPS)
    eye = jnp.eye(S5_BLOCK_GROUPS, dtype=F32)

    def in_map(bbt):
        t = bbt.reshape(2, nsb, S5_BLOCK_GROUPS, S5_GROUP, S5_STATE)
        t = jnp.einsum('fsihp,ij->sfihjp', t, eye)
        return t.reshape(nsb, 2 * S5_BLOCK_U, S5_BLOCK_S)

    def out_map(c):
        t = c.reshape(2, nsb, S5_BLOCK_GROUPS, S5_GROUP, S5_STATE)
        t = jnp.einsum('fsjhp,ij->sipfjh', t, eye)
        return t.reshape(nsb, S5_BLOCK_S, 2 * S5_BLOCK_U)

    def rows8(v, width):
        t = v.reshape(2, nsb, 1, width).transpose(1, 0, 2, 3)
        return jnp.broadcast_to(t, (nsb, 2, bsz, width)).reshape(nsb, 2 * bsz, width)

    wb = jnp.concatenate([in_map(bbt_re), in_map(bbt_im)], axis=2).astype(BF16)
    wc = jnp.concatenate([out_map(c_re), -out_map(c_im)], axis=1).astype(BF16)
    a8 = jnp.concatenate([rows8(ab_re.reshape(-1), S5_BLOCK_S), rows8(ab_im.reshape(-1), S5_BLOCK_S)], axis=2)
    d8 = rows8(d_skip, S5_BLOCK_U)
    return wb, wc, a8, d8


def kernel(x, c, w_ada, b_ada, w_in, w_gla_gate, b_gla_gate, gla_norm_g, s5_lambda_re, s5_lambda_im, s5_log_dt, s5_b_re, s5_b_im, s5_c_re, s5_c_im, s5_d, w_glu, b_glu, w_out, ln_g, ln_b):
    bsz, seq, d = x.shape
    assert bsz * 2 == SUBLANES and w_ada.shape[0] == DEPTH and s5_d.shape[1] == D_S5
    layer = 0

    mod = _ada(c.T, w_ada[layer], b_ada[layer][None, :])
    mod3 = mod.reshape(bsz, 1, 3 * d)

    w_main, w_glr = _wprep(jnp.swapaxes(w_in, 1, 2)[layer])
    proj, g_lr = _inproj(x, mod3, w_main, w_glr)

    wg = jnp.pad(w_gla_gate[layer], ((0, LANES - GLA_GATE_RANK), (0, 0))).astype(BF16)
    o_gla = _gla(proj, g_lr, wg, b_gla_gate[layer][None, :], gla_norm_g[layer][None, :], bsz, seq)

    bt_re = s5_b_re[layer].transpose(0, 2, 1)
    bt_im = s5_b_im[layer].transpose(0, 2, 1)
    ab_re, ab_im, bbt_re, bbt_im = _s5prep(s5_lambda_re[layer], s5_lambda_im[layer], s5_log_dt[layer],
                                            bt_re, bt_im)
    wb, wc, a8, d8 = _s5_layouts(ab_re, ab_im, bbt_re, bbt_im, s5_c_re[layer], s5_c_im[layer],
                                 s5_d[layer], bsz)
    yv = _s5(proj.reshape(bsz, seq, N_PROJ), wb, wc, a8, d8)
    o_s5 = _glu(yv, proj, w_glu[layer].astype(BF16), b_glu[layer][None, :], bsz, seq)

    return _outproj(o_gla, o_s5, w_out[layer].astype(BF16), x, mod3,
                    ln_g[layer][None, :], ln_b[layer][None, :])
```

```python
import functools
import itertools
import math

import jax
import jax.numpy as jnp
from jax import lax
from jax.experimental import pallas as pl
from jax.experimental.pallas import tpu as pltpu

F32 = jnp.float32
BF16 = jnp.bfloat16

GLA_HEADS = 4
GLA_DK = 256
GLA_DV = 512
GLA_GATE_RANK = 16
GLA_GATE_TAU = 16.0
GLA_CHUNK = 64
S5_GROUP = 16
S5_STATE = 64
NORM_EPS = 1e-5
DEPTH = 1
DEEPNORM_ALPHA = (2.0 * DEPTH) ** 0.25

LANES = 128
SUBLANES = 8
VMEM_LIMIT = 56 * 1024 * 1024

DK_TOT = GLA_HEADS * GLA_DK
D_GLA = GLA_HEADS * GLA_DV
D_S5 = 2048
COL_Q = 0
COL_K = COL_Q + DK_TOT
COL_V = COL_K + DK_TOT
COL_ZG = COL_V + D_GLA
COL_U = COL_ZG + D_GLA
COL_ZS = COL_U + D_S5
N_PROJ = COL_ZS + D_S5
MXU_WIDTH = 256
INPROJ_TN = 5 * MXU_WIDTH

S5_BLOCK_GROUPS = 8
S5_BLOCK_U = S5_BLOCK_GROUPS * S5_GROUP
S5_BLOCK_S = S5_BLOCK_GROUPS * S5_STATE
S5_SUB = 16


def _sigmoid(v):
    return 1.0 / (1.0 + jnp.exp(-v))


def _silu(v):
    return v * _sigmoid(v)


def _dot(a, b):
    return jnp.dot(a, b, preferred_element_type=F32)


def _dot_nt(a, b):
    return lax.dot_general(a, b, (((1,), (1,)), ((), ())), preferred_element_type=F32)


def _dot_tn(a, b):
    return lax.dot_general(a, b, (((0,), (0,)), ((), ())), preferred_element_type=F32)


def _params(*sem):
    return pltpu.CompilerParams(dimension_semantics=sem, vmem_limit_bytes=VMEM_LIMIT)


def _ada_kernel(ct_ref, w_ref, b_ref, o_ref, sc_ref):
    nb = sc_ref.shape[0]
    d, tn = w_ref.shape

    @pl.when(pl.program_id(0) == 0)
    def _():
        sc = _silu(ct_ref[...])
        for b in range(nb):
            sc_ref[b] = jnp.broadcast_to(sc[:, b:b + 1], (d, LANES))

    def body(g, accs):
        r0 = pl.multiple_of(g * SUBLANES, SUBLANES)
        w = w_ref[pl.ds(r0, SUBLANES), :]
        return tuple(acc + w * jnp.concatenate([sc_ref[b, pl.ds(r0, SUBLANES), :]] * (tn // LANES), axis=1)
                     for b, acc in enumerate(accs))

    accs = lax.fori_loop(0, d // SUBLANES, body,
                         tuple(jnp.zeros((SUBLANES, tn), F32) for _ in range(nb)), unroll=8)
    for b in range(nb):
        o_ref[b:b + 1, :] = jnp.sum(accs[b], axis=0, keepdims=True) + b_ref[...]


def _ada(ct, w, b):
    d, n = w.shape
    nb = ct.shape[1]
    tn = 512
    return pl.pallas_call(
        _ada_kernel,
        grid=(n // tn,),
        in_specs=[pl.BlockSpec((d, nb), lambda j: (0, 0)),
                  pl.BlockSpec((d, tn), lambda j: (0, j)),
                  pl.BlockSpec((1, tn), lambda j: (0, j))],
        out_specs=pl.BlockSpec((nb, tn), lambda j: (0, j)),
        out_shape=jax.ShapeDtypeStruct((nb, n), F32),
        scratch_shapes=[pltpu.VMEM((nb, d, LANES), F32)],
        compiler_params=_params("arbitrary"),
        name="ada",
    )(ct, w, b)


WPREP_ROWS = 512


def _wprep_kernel(cur_ref, nxt_ref, o_ref, g_ref, *, n_plain):
    i = pl.program_id(0)
    keep = WPREP_ROWS - GLA_GATE_RANK

    @pl.when(i < n_plain)
    def _():
        o_ref[...] = cur_ref[...].astype(BF16)

    @pl.when(i >= n_plain)
    def _():
        o_ref[:keep] = cur_ref[GLA_GATE_RANK:].astype(BF16)
        o_ref[keep:] = nxt_ref[...].astype(BF16)

    @pl.when(i == n_plain - 1)
    def _():
        g_ref[:GLA_GATE_RANK] = nxt_ref[...].astype(BF16)
        g_ref[GLA_GATE_RANK:] = jnp.zeros((LANES - GLA_GATE_RANK, g_ref.shape[1]), BF16)


def _wprep(wt):
    n, d = wt.shape
    assert n == N_PROJ + GLA_GATE_RANK and COL_ZG % WPREP_ROWS == 0 and N_PROJ % WPREP_ROWS == 0
    per = WPREP_ROWS // GLA_GATE_RANK
    return pl.pallas_call(
        functools.partial(_wprep_kernel, n_plain=COL_ZG // WPREP_ROWS),
        grid=(N_PROJ // WPREP_ROWS,),
        in_specs=[pl.BlockSpec((WPREP_ROWS, d), lambda i: (i, 0)),
                  pl.BlockSpec((GLA_GATE_RANK, d), lambda i: ((i + 1) * per, 0))],
        out_specs=(pl.BlockSpec((WPREP_ROWS, d), lambda i: (i, 0)),
                   pl.BlockSpec((LANES, d), lambda i: (0, 0))),
        out_shape=(jax.ShapeDtypeStruct((N_PROJ, d), BF16),
                   jax.ShapeDtypeStruct((LANES, d), BF16)),
        compiler_params=_params("arbitrary"),
        name="wprep",
    )(wt, wt)


def _inproj_kernel(x_ref, shift_ref, scale_ref, w_ref, wg_ref, o_ref, g_ref, h_ref):
    @pl.when(pl.program_id(2) == 0)
    def _():
        h_ref[...] = (x_ref[0] * (1.0 + scale_ref[0]) + shift_ref[0]).astype(BF16)
        g_ref[...] = _dot_nt(h_ref[...], wg_ref[...]).astype(g_ref.dtype)

    o_ref[...] = _dot_nt(h_ref[...], w_ref[...]).astype(o_ref.dtype)


def _inproj(x, mod3, w, wg):
    bsz, seq, d = x.shape
    n = w.shape[0]
    tn = INPROJ_TN
    tm = min(512, seq)
    nt = seq // tm
    return pl.pallas_call(
        _inproj_kernel,
        grid=(bsz, nt, n // tn),
        in_specs=[pl.BlockSpec((1, tm, d), lambda b, i, j: (b, i, 0)),
                  pl.BlockSpec((1, 1, d), lambda b, i, j: (b, 0, 0)),
                  pl.BlockSpec((1, 1, d), lambda b, i, j: (b, 0, 1)),
                  pl.BlockSpec((tn, d), lambda b, i, j: (j, 0)),
                  pl.BlockSpec((LANES, d), lambda b, i, j: (0, 0))],
        out_specs=(pl.BlockSpec((tm, tn), lambda b, i, j: (b * nt + i, j)),
                   pl.BlockSpec((tm, LANES), lambda b, i, j: (b * nt + i, 0))),
        out_shape=(jax.ShapeDtypeStruct((bsz * seq, n), BF16),
                   jax.ShapeDtypeStruct((bsz * seq, LANES), BF16)),
        scratch_shapes=[pltpu.VMEM((tm, d), BF16)],
        compiler_params=_params("parallel", "parallel", "arbitrary"),
        name="inproj",
    )(x, mod3, mod3, w, wg)


GLA_BLOCK = 4 * GLA_CHUNK
GLA_NCHUNK = GLA_BLOCK // GLA_CHUNK
GLA_STEP_BLOCKS = 2


def _gla_constants():
    import numpy as np
    tc = GLA_BLOCK
    row = np.arange(tc)[:, None]
    col = np.arange(tc)[None, :]
    rc = row // GLA_CHUNK
    cc = col // GLA_CHUNK
    half = GLA_NCHUNK // 2
    same = rc == cc
    totals = np.arange(2 * SUBLANES)[:, None] == cc
    sums = np.concatenate([same & (col <= row), totals], axis=0)
    sel = np.where(same & (col <= row), 1.0,
                   np.where((rc == cc + 1) & (rc != half), 2.0,
                            np.where((rc >= half) & (cc < half), 3.0, 0.0)))
    return jnp.asarray(sums.astype(np.float32), dtype=BF16), jnp.asarray(sel.astype(np.float32))


def _gla_kernel(q_ref, k_ref, g_ref, v_ref, z_ref, wg_ref, bg_ref, ng_ref, sums_ref, sel_ref, o_ref, st_ref):
    @pl.when(pl.program_id(1) == 0)
    def _():
        st_ref[...] = jnp.zeros_like(st_ref)

    tc = GLA_BLOCK
    sel = sel_ref[...]
    logit = _dot(g_ref[...], wg_ref[...]) + bg_ref[...]
    la_all = (jnp.minimum(logit, 0.0) - jnp.log(1.0 + jnp.exp(-jnp.abs(logit)))) * (1.0 / GLA_GATE_TAU)
    for blk, hd in itertools.product(range(GLA_STEP_BLOCKS), range(GLA_HEADS)):
        rs = slice(blk * tc, (blk + 1) * tc)
        ck = slice(hd * GLA_DK, (hd + 1) * GLA_DK)
        cv = slice(hd * GLA_DV, (hd + 1) * GLA_DV)
        la = la_all[rs, ck]
        la_hi = la.astype(BF16)
        la_lo = (la - la_hi.astype(F32)).astype(BF16)
        sums = _dot(sums_ref[...], la_hi) + _dot(sums_ref[...], la_lo)
        b = sums[:tc]
        tot = [sums[tc + c:tc + c + 1] for c in range(GLA_NCHUNK)]
        half = GLA_NCHUNK // 2

        def span(lo, hi):
            return sum(tot[lo:hi]) if hi > lo else jnp.zeros_like(tot[0])

        k = k_ref[rs, ck].astype(F32)
        q_dec = q_ref[rs, ck].astype(F32) * (GLA_DK ** -0.5) * jnp.exp(b)
        k_inv = k * jnp.exp(-b)
        k_end, q_in, k_st, q_mid, k_mid = [], [], [], [], []
        for c in range(GLA_NCHUNK):
            rows = slice(c * GLA_CHUNK, (c + 1) * GLA_CHUNK)
            ke = k[rows] * jnp.exp(tot[c] - b[rows])
            e_mid = jnp.exp(span(half, c) if c >= half else span(c + 1, half))
            k_end.append(ke)
            q_in.append(q_dec[rows] * jnp.exp(span(0, c)))
            k_st.append(ke * jnp.exp(span(c + 1, GLA_NCHUNK)))
            q_mid.append(q_dec[rows] * e_mid)
            k_mid.append(ke * e_mid)
        cat = lambda parts: jnp.concatenate(parts, axis=0).astype(BF16)
        a_same = _dot_nt(q_dec.astype(BF16), k_inv.astype(BF16))
        a_next = _dot_nt(q_dec.astype(BF16), cat(k_end))
        a_mid = _dot_nt(cat(q_mid), cat(k_mid))
        att = jnp.where(sel == 1.0, a_same,
                        jnp.where(sel == 2.0, a_next, jnp.where(sel == 3.0, a_mid, 0.0))).astype(BF16)
        v = v_ref[rs, cv]
        st = st_ref[hd]
        o = _dot(att, v) + _dot_nt(cat(q_in), st.astype(BF16))
        st_ref[hd] = st * jnp.exp(span(0, GLA_NCHUNK)) + _dot_tn(v, cat(k_st))
        o = o * lax.rsqrt(jnp.mean(o * o, axis=-1, keepdims=True) + NORM_EPS) * ng_ref[...]
        z = z_ref[rs, cv].astype(F32)
        o_ref[rs, cv] = (o * _silu(z)).astype(o_ref.dtype)


def _gla(proj, g_lr, wg, bg, ng, bsz, seq):
    tc = GLA_BLOCK * GLA_STEP_BLOCKS
    nt = seq // tc
    row = lambda b, i: b * nt + i
    sums, sel = _gla_constants()
    const = lambda b, i: (0, 0)
    return pl.pallas_call(
        _gla_kernel,
        grid=(bsz, nt),
        in_specs=[pl.BlockSpec((tc, DK_TOT), lambda b, i: (row(b, i), COL_Q // DK_TOT)),
                  pl.BlockSpec((tc, DK_TOT), lambda b, i: (row(b, i), COL_K // DK_TOT)),
                  pl.BlockSpec((tc, LANES), lambda b, i: (row(b, i), 0)),
                  pl.BlockSpec((tc, D_GLA), lambda b, i: (row(b, i), COL_V // D_GLA)),
                  pl.BlockSpec((tc, D_GLA), lambda b, i: (row(b, i), COL_ZG // D_GLA)),
                  pl.BlockSpec((LANES, DK_TOT), const),
                  pl.BlockSpec((1, DK_TOT), const),
                  pl.BlockSpec((1, GLA_DV), const),
                  pl.BlockSpec((GLA_BLOCK + 2 * SUBLANES, GLA_BLOCK), const),
                  pl.BlockSpec((GLA_BLOCK, GLA_BLOCK), const)],
        out_specs=pl.BlockSpec((tc, D_GLA), lambda b, i: (row(b, i), 0)),
        out_shape=jax.ShapeDtypeStruct((bsz * seq, D_GLA), BF16),
        scratch_shapes=[pltpu.VMEM((GLA_HEADS, GLA_DV, GLA_DK), F32)],
        compiler_params=_params("parallel", "arbitrary"),
        name="gla",
    )(proj, proj, g_lr, proj, proj, wg, bg, ng, sums, sel)


def _s5prep_kernel(lre_ref, lim_ref, ldt_ref, bre_ref, bim_ref, are_ref, aim_ref, bbre_ref, bbim_ref):
    lre = lre_ref[...]
    lim = lim_ref[...]
    dt = jnp.exp(ldt_ref[...])
    z_re = lre * dt
    z_im = lim * dt
    mag = jnp.exp(z_re)
    ab_re = mag * jnp.cos(z_im)
    ab_im = mag * jnp.sin(z_im)
    den = lre * lre + lim * lim
    n_re = ab_re - 1.0
    n_im = ab_im
    f_re = (n_re * lre + n_im * lim) / den
    f_im = (n_im * lre - n_re * lim) / den
    b_re = bre_ref[...]
    b_im = bim_ref[...]
    are_ref[...] = ab_re
    aim_ref[...] = ab_im
    bbre_ref[...] = f_re * b_re - f_im * b_im
    bbim_ref[...] = f_re * b_im + f_im * b_re


def _s5prep(lam_re, lam_im, log_dt, bt_re, bt_im):
    g, p = lam_re.shape
    hh = bt_re.shape[1]
    a_shape = jax.ShapeDtypeStruct((g, 1, p), F32)
    b_shape = jax.ShapeDtypeStruct((g, hh, p), F32)
    return pl.pallas_call(
        _s5prep_kernel,
        out_shape=(a_shape, a_shape, b_shape, b_shape),
        name="s5prep",
    )(lam_re.reshape(g, 1, p), lam_im.reshape(g, 1, p), log_dt.reshape(g, 1, 1), bt_re, bt_im)


def _s5_kernel(u0_ref, u1_ref, wb_ref, wc_ref, a_ref, d_ref, o_ref, uf_ref, bu_ref, s_ref, y_ref, carry_ref, *, tc):
    @pl.when(pl.program_id(1) == 0)
    def _():
        carry_ref[...] = jnp.zeros_like(carry_ref)

    nb = u0_ref.shape[0]
    tile = SUBLANES * S5_SUB
    for b in range(nb):
        uf_ref[pl.ds(b, tc, stride=SUBLANES), :] = u0_ref[b].astype(F32)
        uf_ref[pl.ds(nb + b, tc, stride=SUBLANES), :] = u1_ref[b].astype(F32)

    lo = (lax.broadcasted_iota(jnp.int32, (tile, S5_BLOCK_U), 0) & nb) == 0
    a_re = a_ref[0, :, :S5_BLOCK_S]
    a_im = a_ref[0, :, S5_BLOCK_S:]
    s_re = carry_ref[:, :S5_BLOCK_S]
    s_im = carry_ref[:, S5_BLOCK_S:]

    for s in range(tc // S5_SUB):
        base = s * tile
        uf = uf_ref[base:base + tile, :]
        lhs = jnp.concatenate([jnp.where(lo, uf, 0.0), jnp.where(lo, 0.0, uf)], axis=1).astype(BF16)
        bu_ref[base:base + tile, :] = _dot(lhs, wb_ref[0])
        for p in range(S5_SUB // 2):
            r0 = base + p * 2 * SUBLANES
            r1 = r0 + SUBLANES
            m_re = a_re * s_re - a_im * s_im + bu_ref[r0:r1, :S5_BLOCK_S]
            m_im = a_re * s_im + a_im * s_re + bu_ref[r0:r1, S5_BLOCK_S:]
            s_re = a_re * m_re - a_im * m_im + bu_ref[r1:r1 + SUBLANES, :S5_BLOCK_S]
            s_im = a_re * m_im + a_im * m_re + bu_ref[r1:r1 + SUBLANES, S5_BLOCK_S:]
            s_ref[r0:r0 + 2 * SUBLANES, :S5_BLOCK_S] = jnp.concatenate([m_re, s_re], axis=0).astype(BF16)
            s_ref[r0:r0 + 2 * SUBLANES, S5_BLOCK_S:] = jnp.concatenate([m_im, s_im], axis=0).astype(BF16)
        y8 = _dot(s_ref[base:base + tile, :], wc_ref[0])
        y = jnp.where(lo, y8[:, :S5_BLOCK_U], y8[:, S5_BLOCK_U:])
        y = (y.reshape(S5_SUB, SUBLANES, S5_BLOCK_U)
             + d_ref[...] * uf.reshape(S5_SUB, SUBLANES, S5_BLOCK_U)).reshape(tile, S5_BLOCK_U)
        cdf = 0.5 * (1.0 + jnp.tanh(math.sqrt(2.0 / math.pi) * (y + 0.044715 * (y * y * y))))
        y_ref[base:base + tile, :] = y * cdf

    carry_ref[:, :S5_BLOCK_S] = s_re
    carry_ref[:, S5_BLOCK_S:] = s_im
    for j in range(2 * nb):
        o_ref[j] = y_ref[pl.ds(j, tc, stride=SUBLANES), :].astype(o_ref.dtype)


def _s5(proj3, wb, wc, a8, d8):
    bsz, seq, _ = proj3.shape
    nsb = wb.shape[0]
    tc = min(512, seq)
    rows = SUBLANES * tc
    cb = COL_U // S5_BLOCK_U
    return pl.pallas_call(
        functools.partial(_s5_kernel, tc=tc),
        grid=(nsb, seq // tc),
        in_specs=[pl.BlockSpec((bsz, tc, S5_BLOCK_U), lambda sb, i: (0, i, cb + sb)),
                  pl.BlockSpec((bsz, tc, S5_BLOCK_U), lambda sb, i: (0, i, cb + nsb + sb)),
                  pl.BlockSpec((1, 2 * S5_BLOCK_U, 2 * S5_BLOCK_S), lambda sb, i: (sb, 0, 0)),
                  pl.BlockSpec((1, 2 * S5_BLOCK_S, 2 * S5_BLOCK_U), lambda sb, i: (sb, 0, 0)),
                  pl.BlockSpec((1, SUBLANES, 2 * S5_BLOCK_S), lambda sb, i: (sb, 0, 0)),
                  pl.BlockSpec((1, SUBLANES, S5_BLOCK_U), lambda sb, i: (sb, 0, 0))],
        out_specs=pl.BlockSpec((2 * bsz, tc, S5_BLOCK_U), lambda sb, i: (0, i, sb)),
        out_shape=jax.ShapeDtypeStruct((2 * bsz, seq, nsb * S5_BLOCK_U), BF16),
        scratch_shapes=[pltpu.VMEM((rows, S5_BLOCK_U), F32),
                        pltpu.VMEM((rows, 2 * S5_BLOCK_S), F32),
                        pltpu.VMEM((rows, 2 * S5_BLOCK_S), BF16),
                        pltpu.VMEM((rows, S5_BLOCK_U), F32),
                        pltpu.VMEM((SUBLANES, 2 * S5_BLOCK_S), F32)],
        compiler_params=_params("parallel", "arbitrary"),
        name="s5",
    )(proj3, proj3, wb, wc, a8, d8)


GLU_ROW_CHUNKS = 2


def _glu_kernel(y0_ref, y1_ref, z_ref, w_ref, b_ref, o_ref):
    tm, half = y0_ref.shape[1:]
    rows = tm // GLU_ROW_CHUNKS
    for c in range(GLU_ROW_CHUNKS):
        rs = slice(c * rows, (c + 1) * rows)
        y0 = y0_ref[0, rs, :]
        y1 = y1_ref[0, rs, :]
        acc = _dot(y0, w_ref[:half, :]) + _dot(y1, w_ref[half:, :]) + b_ref[...]
        y = jnp.concatenate([y0, y1], axis=1).astype(F32)
        z = z_ref[rs, :].astype(F32)
        o_ref[rs, :] = (y * _sigmoid(acc) * _silu(z)).astype(o_ref.dtype)


def _glu(yv, proj, w, b, bsz, seq):
    d = w.shape[0]
    half = d // 2
    tm = min(512, seq)
    nt = seq // tm
    return pl.pallas_call(
        _glu_kernel,
        grid=(bsz, nt),
        in_specs=[pl.BlockSpec((1, tm, half), lambda b_, i: (b_, i, 0)),
                  pl.BlockSpec((1, tm, half), lambda b_, i: (bsz + b_, i, 0)),
                  pl.BlockSpec((tm, d), lambda b_, i: (b_ * nt + i, COL_ZS // D_S5)),
                  pl.BlockSpec((d, d), lambda b_, i: (0, 0)),
                  pl.BlockSpec((1, d), lambda b_, i: (0, 0))],
        out_specs=pl.BlockSpec((tm, d), lambda b_, i: (b_ * nt + i, 0)),
        out_shape=jax.ShapeDtypeStruct((bsz * seq, d), BF16),
        compiler_params=_params("parallel", "parallel"),
        name="glu",
    )(yv, yv, proj, w, b)


OUT_EPILOGUE_CHUNKS = 2


def _out_kernel(og_ref, os_ref, w_ref, x_hbm, gate_ref, lg_ref, lb_ref, o_ref, x_buf, x_sem, *, nk, tm):
    b = pl.program_id(0)
    i = pl.program_id(1)
    k = pl.program_id(2)
    half = nk // 2

    def x_copy():
        r0 = pl.multiple_of(i * tm, tm)
        return pltpu.make_async_copy(x_hbm.at[b, pl.ds(r0, tm), :], x_buf, x_sem)

    @pl.when(k == 0)
    def _():
        x_copy().start()
        o_ref[0] = _dot(og_ref[...], w_ref[...])

    @pl.when((k > 0) & (k < half))
    def _():
        o_ref[0] += _dot(og_ref[...], w_ref[...])

    @pl.when((k >= half) & (k < nk - 1))
    def _():
        o_ref[0] += _dot(os_ref[...], w_ref[...])

    @pl.when(k == nk - 1)
    def _():
        x_copy().wait()
        rows = tm // OUT_EPILOGUE_CHUNKS
        for c in range(OUT_EPILOGUE_CHUNKS):
            rs = slice(c * rows, (c + 1) * rows)
            mixed = o_ref[0, rs, :] + _dot(os_ref[rs, :], w_ref[...])
            r = DEEPNORM_ALPHA * x_buf[rs, :] + gate_ref[0] * mixed
            mu = jnp.mean(r, axis=-1, keepdims=True)
            rc = r - mu
            var = jnp.mean(rc * rc, axis=-1, keepdims=True)
            o_ref[0, rs, :] = rc * lax.rsqrt(var + NORM_EPS) * lg_ref[...] + lb_ref[...]


def _outproj(og, osb, w, x, mod3, lg, lb):
    bsz, seq, d = x.shape
    dh = og.shape[1]
    tm = min(512, seq)
    nt = seq // tm
    tk = 1024
    half = dh // tk
    nk = 2 * half
    return pl.pallas_call(
        functools.partial(_out_kernel, nk=nk, tm=tm),
        grid=(bsz, nt, nk),
        in_specs=[pl.BlockSpec((tm, tk), lambda b, i, k: (b * nt + i, jnp.minimum(k, half - 1))),
                  pl.BlockSpec((tm, tk), lambda b, i, k: (b * nt + i, jnp.maximum(k - half, 0))),
                  pl.BlockSpec((tk, d), lambda b, i, k: (k, 0)),
                  pl.BlockSpec(memory_space=pl.ANY),
                  pl.BlockSpec((1, 1, d), lambda b, i, k: (b, 0, 2)),
                  pl.BlockSpec((1, d), lambda b, i, k: (0, 0)),
                  pl.BlockSpec((1, d), lambda b, i, k: (0, 0))],
        out_specs=pl.BlockSpec((1, tm, d), lambda b, i, k: (b, i, 0)),
        out_shape=jax.ShapeDtypeStruct(x.shape, x.dtype),
        scratch_shapes=[pltpu.VMEM((tm, d), F32), pltpu.SemaphoreType.DMA(())],
        compiler_params=_params("parallel", "parallel", "arbitrary"),
        name="outproj",
    )(og, osb, w, x, mod3, lg, lb)


def _s5_layouts(ab_re, ab_im, bbt_re, bbt_im, c_re, c_im, d_skip, bsz):
    g = ab_re.shape[0]
    nsb = g // (2 * S5_BLOCK_GROUPS)
    eye = jnp.eye(S5_BLOCK_GROUPS, dtype=F32)

    def in_map(bbt):
        t = bbt.reshape(2, nsb, S5_BLOCK_GROUPS, S5_GROUP, S5_STATE)
        t = jnp.einsum('fsihp,ij->sfihjp', t, eye)
        return t.reshape(nsb, 2 * S5_BLOCK_U, S5_BLOCK_S)

    def out_map(c):
        t = c.reshape(2, nsb, S5_BLOCK_GROUPS, S5_GROUP, S5_STATE)
        t = jnp.einsum('fsjhp,ij->sipfjh', t, eye)
        return t.reshape(nsb, S5_BLOCK_S, 2 * S5_BLOCK_U)

    def rows8(v, width):
        t = v.reshape(2, nsb, 1, width).transpose(1, 0, 2, 3)
        return jnp.broadcast_to(t, (nsb, 2, bsz, width)).reshape(nsb, 2 * bsz, width)

    wb = jnp.concatenate([in_map(bbt_re), in_map(bbt_im)], axis=2).astype(BF16)
    wc = jnp.concatenate([out_map(c_re), -out_map(c_im)], axis=1).astype(BF16)
    a8 = jnp.concatenate([rows8(ab_re.reshape(-1), S5_BLOCK_S), rows8(ab_im.reshape(-1), S5_BLOCK_S)], axis=2)
    d8 = rows8(d_skip, S5_BLOCK_U)
    return wb, wc, a8, d8


def kernel(x, c, w_ada, b_ada, w_in, w_gla_gate, b_gla_gate, gla_norm_g, s5_lambda_re, s5_lambda_im, s5_log_dt, s5_b_re, s5_b_im, s5_c_re, s5_c_im, s5_d, w_glu, b_glu, w_out, ln_g, ln_b):
    bsz, seq, d = x.shape
    assert bsz * 2 == SUBLANES and w_ada.shape[0] == DEPTH and s5_d.shape[1] == D_S5
    layer = 0

    mod = _ada(c.T, w_ada[layer], b_ada[layer][None, :])
    mod3 = mod.reshape(bsz, 1, 3 * d)

    w_main, w_glr = _wprep(jnp.swapaxes(w_in, 1, 2)[layer])
    proj, g_lr = _inproj(x, mod3, w_main, w_glr)

    wg = jnp.pad(w_gla_gate[layer], ((0, LANES - GLA_GATE_RANK), (0, 0))).astype(BF16)
    o_gla = _gla(proj, g_lr, wg, b_gla_gate[layer][None, :], gla_norm_g[layer][None, :], bsz, seq)

    bt_re = s5_b_re[layer].transpose(0, 2, 1)
    bt_im = s5_b_im[layer].transpose(0, 2, 1)
    ab_re, ab_im, bbt_re, bbt_im = _s5prep(s5_lambda_re[layer], s5_lambda_im[layer], s5_log_dt[layer],
                                            bt_re, bt_im)
    wb, wc, a8, d8 = _s5_layouts(ab_re, ab_im, bbt_re, bbt_im, s5_c_re[layer], s5_c_im[layer],
                                 s5_d[layer], bsz)
    yv = _s5(proj.reshape(bsz, seq, N_PROJ), wb, wc, a8, d8)
    o_s5 = _glu(yv, proj, w_glu[layer].astype(BF16), b_glu[layer][None, :], bsz, seq)

    return _outproj(o_gla, o_s5, w_out[layer].astype(BF16), x, mod3,
                    ln_g[layer][None, :], ln_b[layer][None, :])
```

```python
import functools
import itertools
import math

import jax
import jax.numpy as jnp
from jax import lax
from jax.experimental import pallas as pl
from jax.experimental.pallas import tpu as pltpu

F32 = jnp.float32
BF16 = jnp.bfloat16

GLA_HEADS = 4
GLA_DK = 256
GLA_DV = 512
GLA_GATE_RANK = 16
GLA_GATE_TAU = 16.0
GLA_CHUNK = 64
S5_GROUP = 16
S5_STATE = 64
NORM_EPS = 1e-5
DEPTH = 1
DEEPNORM_ALPHA = (2.0 * DEPTH) ** 0.25

LANES = 128
SUBLANES = 8
VMEM_LIMIT = 60 * 1024 * 1024

DK_TOT = GLA_HEADS * GLA_DK
D_GLA = GLA_HEADS * GLA_DV
D_S5 = 2048
COL_Q = 0
COL_K = COL_Q + DK_TOT
COL_V = COL_K + DK_TOT
COL_ZG = COL_V + D_GLA
COL_U = COL_ZG + D_GLA
COL_ZS = COL_U + D_S5
N_PROJ = COL_ZS + D_S5
MXU_WIDTH = 256
INPROJ_TN = 5 * MXU_WIDTH

S5_BLOCK_GROUPS = 8
S5_BLOCK_U = S5_BLOCK_GROUPS * S5_GROUP
S5_BLOCK_S = S5_BLOCK_GROUPS * S5_STATE
S5_SUB = 16


def _sigmoid(v):
    return 1.0 / (1.0 + jnp.exp(-v))


def _silu(v):
    return v * _sigmoid(v)


def _dot(a, b):
    return jnp.dot(a, b, preferred_element_type=F32)


def _dot_nt(a, b):
    return lax.dot_general(a, b, (((1,), (1,)), ((), ())), preferred_element_type=F32)


def _dot_tn(a, b):
    return lax.dot_general(a, b, (((0,), (0,)), ((), ())), preferred_element_type=F32)


def _params(*sem):
    return pltpu.CompilerParams(dimension_semantics=sem, vmem_limit_bytes=VMEM_LIMIT)


def _ada_kernel(ct_ref, w_ref, b_ref, o_ref, sc_ref):
    nb = sc_ref.shape[0]
    d, tn = w_ref.shape

    @pl.when(pl.program_id(0) == 0)
    def _():
        sc = _silu(ct_ref[...])
        for b in range(nb):
            sc_ref[b] = jnp.broadcast_to(sc[:, b:b + 1], (d, LANES))

    def body(g, accs):
        r0 = pl.multiple_of(g * SUBLANES, SUBLANES)
        w = w_ref[pl.ds(r0, SUBLANES), :]
        return tuple(acc + w * jnp.concatenate([sc_ref[b, pl.ds(r0, SUBLANES), :]] * (tn // LANES), axis=1)
                     for b, acc in enumerate(accs))

    accs = lax.fori_loop(0, d // SUBLANES, body,
                         tuple(jnp.zeros((SUBLANES, tn), F32) for _ in range(nb)), unroll=8)
    for b in range(nb):
        o_ref[b:b + 1, :] = jnp.sum(accs[b], axis=0, keepdims=True) + b_ref[...]


def _ada(ct, w, b):
    d, n = w.shape
    nb = ct.shape[1]
    tn = 1024
    return pl.pallas_call(
        _ada_kernel,
        grid=(n // tn,),
        in_specs=[pl.BlockSpec((d, nb), lambda j: (0, 0)),
                  pl.BlockSpec((d, tn), lambda j: (0, j)),
                  pl.BlockSpec((1, tn), lambda j: (0, j))],
        out_specs=pl.BlockSpec((nb, tn), lambda j: (0, j)),
        out_shape=jax.ShapeDtypeStruct((nb, n), F32),
        scratch_shapes=[pltpu.VMEM((nb, d, LANES), F32)],
        compiler_params=_params("arbitrary"),
        name="ada",
    )(ct, w, b)


WPREP_ROWS = 512


def _wprep_kernel(cur_ref, nxt_ref, o_ref, g_ref, *, n_plain):
    i = pl.program_id(0)
    keep = WPREP_ROWS - GLA_GATE_RANK

    @pl.when(i < n_plain)
    def _():
        o_ref[...] = cur_ref[...].astype(BF16)

    @pl.when(i >= n_plain)
    def _():
        o_ref[:keep] = cur_ref[GLA_GATE_RANK:].astype(BF16)
        o_ref[keep:] = nxt_ref[...].astype(BF16)

    @pl.when(i == n_plain - 1)
    def _():
        g_ref[:GLA_GATE_RANK] = nxt_ref[...].astype(BF16)
        g_ref[GLA_GATE_RANK:] = jnp.zeros((LANES - GLA_GATE_RANK, g_ref.shape[1]), BF16)


def _wprep(wt):
    n, d = wt.shape
    assert n == N_PROJ + GLA_GATE_RANK and COL_ZG % WPREP_ROWS == 0 and N_PROJ % WPREP_ROWS == 0
    per = WPREP_ROWS // GLA_GATE_RANK
    return pl.pallas_call(
        functools.partial(_wprep_kernel, n_plain=COL_ZG // WPREP_ROWS),
        grid=(N_PROJ // WPREP_ROWS,),
        in_specs=[pl.BlockSpec((WPREP_ROWS, d), lambda i: (i, 0)),
                  pl.BlockSpec((GLA_GATE_RANK, d), lambda i: ((i + 1) * per, 0))],
        out_specs=(pl.BlockSpec((WPREP_ROWS, d), lambda i: (i, 0)),
                   pl.BlockSpec((LANES, d), lambda i: (0, 0))),
        out_shape=(jax.ShapeDtypeStruct((N_PROJ, d), BF16),
                   jax.ShapeDtypeStruct((LANES, d), BF16)),
        compiler_params=_params("arbitrary"),
        name="wprep",
    )(wt, wt)


def _inproj_kernel(x_hbm, shift_ref, scale_ref, w_ref, wg_ref, o_ref, g_ref, h_ref, x_buf, x_sem, *, tm, nt, n_tiles):
    b = pl.program_id(0)
    i = pl.program_id(1)

    def x_copy(tile):
        r0 = pl.multiple_of((tile % nt) * tm, tm)
        return pltpu.make_async_copy(x_hbm.at[tile // nt, pl.ds(r0, tm), :], x_buf, x_sem)

    @pl.when(pl.program_id(2) == 0)
    def _():
        tile = b * nt + i

        @pl.when(tile == 0)
        def _():
            x_copy(tile).start()

        x_copy(tile).wait()
        h_ref[...] = (x_buf[...] * (1.0 + scale_ref[0]) + shift_ref[0]).astype(BF16)

        @pl.when(tile + 1 < n_tiles)
        def _():
            x_copy(tile + 1).start()

        g_ref[...] = _dot_nt(h_ref[...], wg_ref[...]).astype(g_ref.dtype)

    o_ref[...] = _dot_nt(h_ref[...], w_ref[...]).astype(o_ref.dtype)


def _inproj(x, mod3, w, wg):
    bsz, seq, d = x.shape
    n = w.shape[0]
    tn = INPROJ_TN
    tm = min(1024, seq)
    nt = seq // tm
    return pl.pallas_call(
        functools.partial(_inproj_kernel, tm=tm, nt=nt, n_tiles=bsz * nt),
        grid=(bsz, nt, n // tn),
        in_specs=[pl.BlockSpec(memory_space=pl.ANY),
                  pl.BlockSpec((1, 1, d), lambda b, i, j: (b, 0, 0)),
                  pl.BlockSpec((1, 1, d), lambda b, i, j: (b, 0, 1)),
                  pl.BlockSpec((tn, d), lambda b, i, j: (j, 0)),
                  pl.BlockSpec((LANES, d), lambda b, i, j: (0, 0))],
        out_specs=(pl.BlockSpec((tm, tn), lambda b, i, j: (b * nt + i, j)),
                   pl.BlockSpec((tm, LANES), lambda b, i, j: (b * nt + i, 0))),
        out_shape=(jax.ShapeDtypeStruct((bsz * seq, n), BF16),
                   jax.ShapeDtypeStruct((bsz * seq, LANES), BF16)),
        scratch_shapes=[pltpu.VMEM((tm, d), BF16), pltpu.VMEM((tm, d), F32), pltpu.SemaphoreType.DMA(())],
        compiler_params=_params("arbitrary", "arbitrary", "arbitrary"),
        name="inproj",
    )(x, mod3, mod3, w, wg)


GLA_BLOCK = 4 * GLA_CHUNK
GLA_NCHUNK = GLA_BLOCK // GLA_CHUNK
GLA_STEP_BLOCKS = 2


def _gla_constants():
    import numpy as np
    tc = GLA_BLOCK
    row = np.arange(tc)[:, None]
    col = np.arange(tc)[None, :]
    rc = row // GLA_CHUNK
    cc = col // GLA_CHUNK
    half = GLA_NCHUNK // 2
    same = rc == cc
    totals = np.arange(2 * SUBLANES)[:, None] == cc
    sums = np.concatenate([same & (col <= row), totals], axis=0)
    sel = np.where(same & (col <= row), 1.0,
                   np.where((rc == cc + 1) & (rc != half), 2.0,
                            np.where((rc >= half) & (cc < half), 3.0, 0.0)))
    return jnp.asarray(sums.astype(np.float32), dtype=BF16), jnp.asarray(sel.astype(np.float32))


def _gla_kernel(q_ref, k_ref, g_ref, v_ref, z_ref, wg_ref, bg_ref, ng_ref, sums_ref, sel_ref, o_ref, st_ref):
    @pl.when(pl.program_id(1) == 0)
    def _():
        st_ref[...] = jnp.zeros_like(st_ref)

    tc = GLA_BLOCK
    sel = sel_ref[...]
    logit = _dot(g_ref[...], wg_ref[...]) + bg_ref[...]
    la_all = (jnp.minimum(logit, 0.0) - jnp.log(1.0 + jnp.exp(-jnp.abs(logit)))) * (1.0 / GLA_GATE_TAU)
    for blk, hd in itertools.product(range(GLA_STEP_BLOCKS), range(GLA_HEADS)):
        rs = slice(blk * tc, (blk + 1) * tc)
        ck = slice(hd * GLA_DK, (hd + 1) * GLA_DK)
        cv = slice(hd * GLA_DV, (hd + 1) * GLA_DV)
        la = la_all[rs, ck]
        la_hi = la.astype(BF16)
        la_lo = (la - la_hi.astype(F32)).astype(BF16)
        sums = _dot(sums_ref[...], la_hi) + _dot(sums_ref[...], la_lo)
        b = sums[:tc]
        tot = [sums[tc + c:tc + c + 1] for c in range(GLA_NCHUNK)]
        half = GLA_NCHUNK // 2

        def span(lo, hi):
            return sum(tot[lo:hi]) if hi > lo else jnp.zeros_like(tot[0])

        k = k_ref[rs, ck].astype(F32)
        q_dec = q_ref[rs, ck].astype(F32) * (GLA_DK ** -0.5) * jnp.exp(b)
        k_inv = k * jnp.exp(-b)
        k_end, q_in, k_st, q_mid, k_mid = [], [], [], [], []
        for c in range(GLA_NCHUNK):
            rows = slice(c * GLA_CHUNK, (c + 1) * GLA_CHUNK)
            ke = k[rows] * jnp.exp(tot[c] - b[rows])
            e_mid = jnp.exp(span(half, c) if c >= half else span(c + 1, half))
            k_end.append(ke)
            q_in.append(q_dec[rows] * jnp.exp(span(0, c)))
            k_st.append(ke * jnp.exp(span(c + 1, GLA_NCHUNK)))
            q_mid.append(q_dec[rows] * e_mid)
            k_mid.append(ke * e_mid)
        cat = lambda parts: jnp.concatenate(parts, axis=0).astype(BF16)
        a_same = _dot_nt(q_dec.astype(BF16), k_inv.astype(BF16))
        a_next = _dot_nt(q_dec.astype(BF16), cat(k_end))
        a_mid = _dot_nt(cat(q_mid), cat(k_mid))
        att = jnp.where(sel == 1.0, a_same,
                        jnp.where(sel == 2.0, a_next, jnp.where(sel == 3.0, a_mid, 0.0))).astype(BF16)
        v = v_ref[rs, cv]
        st = st_ref[hd]
        o = _dot(att, v) + _dot_nt(cat(q_in), st.astype(BF16))
        st_ref[hd] = st * jnp.exp(span(0, GLA_NCHUNK)) + _dot_tn(v, cat(k_st))
        o = o * lax.rsqrt(jnp.mean(o * o, axis=-1, keepdims=True) + NORM_EPS) * ng_ref[...]
        z = z_ref[rs, cv].astype(F32)
        o_ref[rs, cv] = (o * _silu(z)).astype(o_ref.dtype)


def _gla(proj, g_lr, wg, bg, ng, bsz, seq):
    tc = GLA_BLOCK * GLA_STEP_BLOCKS
    nt = seq // tc
    row = lambda b, i: b * nt + i
    sums, sel = _gla_constants()
    const = lambda b, i: (0, 0)
    return pl.pallas_call(
        _gla_kernel,
        grid=(bsz, nt),
        in_specs=[pl.BlockSpec((tc, DK_TOT), lambda b, i: (row(b, i), COL_Q // DK_TOT)),
                  pl.BlockSpec((tc, DK_TOT), lambda b, i: (row(b, i), COL_K // DK_TOT)),
                  pl.BlockSpec((tc, LANES), lambda b, i: (row(b, i), 0)),
                  pl.BlockSpec((tc, D_GLA), lambda b, i: (row(b, i), COL_V // D_GLA)),
                  pl.BlockSpec((tc, D_GLA), lambda b, i: (row(b, i), COL_ZG // D_GLA)),
                  pl.BlockSpec((LANES, DK_TOT), const),
                  pl.BlockSpec((1, DK_TOT), const),
                  pl.BlockSpec((1, GLA_DV), const),
                  pl.BlockSpec((GLA_BLOCK + 2 * SUBLANES, GLA_BLOCK), const),
                  pl.BlockSpec((GLA_BLOCK, GLA_BLOCK), const)],
        out_specs=pl.BlockSpec((tc, D_GLA), lambda b, i: (row(b, i), 0)),
        out_shape=jax.ShapeDtypeStruct((bsz * seq, D_GLA), BF16),
        scratch_shapes=[pltpu.VMEM((GLA_HEADS, GLA_DV, GLA_DK), F32)],
        compiler_params=_params("parallel", "arbitrary"),
        name="gla",
    )(proj, proj, g_lr, proj, proj, wg, bg, ng, sums, sel)


def _s5prep_kernel(lre_ref, lim_ref, ldt_ref, bre_ref, bim_ref, are_ref, aim_ref, bbre_ref, bbim_ref):
    lre = lre_ref[...]
    lim = lim_ref[...]
    dt = jnp.exp(ldt_ref[...])
    z_re = lre * dt
    z_im = lim * dt
    mag = jnp.exp(z_re)
    ab_re = mag * jnp.cos(z_im)
    ab_im = mag * jnp.sin(z_im)
    den = lre * lre + lim * lim
    n_re = ab_re - 1.0
    n_im = ab_im
    f_re = (n_re * lre + n_im * lim) / den
    f_im = (n_im * lre - n_re * lim) / den
    b_re = bre_ref[...]
    b_im = bim_ref[...]
    are_ref[...] = ab_re
    aim_ref[...] = ab_im
    bbre_ref[...] = f_re * b_re - f_im * b_im
    bbim_ref[...] = f_re * b_im + f_im * b_re


def _s5prep(lam_re, lam_im, log_dt, bt_re, bt_im):
    g, p = lam_re.shape
    hh = bt_re.shape[1]
    a_shape = jax.ShapeDtypeStruct((g, 1, p), F32)
    b_shape = jax.ShapeDtypeStruct((g, hh, p), F32)
    return pl.pallas_call(
        _s5prep_kernel,
        out_shape=(a_shape, a_shape, b_shape, b_shape),
        name="s5prep",
    )(lam_re.reshape(g, 1, p), lam_im.reshape(g, 1, p), log_dt.reshape(g, 1, 1), bt_re, bt_im)


def _s5_kernel(u0_ref, u1_ref, wb_ref, wc_ref, a_ref, d_ref, o_ref, uf_ref, bu_ref, s_ref, y_ref, carry_ref, *, tc):
    @pl.when(pl.program_id(1) == 0)
    def _():
        carry_ref[...] = jnp.zeros_like(carry_ref)

    nb = u0_ref.shape[0]
    tile = SUBLANES * S5_SUB
    for b in range(nb):
        uf_ref[pl.ds(b, tc, stride=SUBLANES), :] = u0_ref[b].astype(F32)
        uf_ref[pl.ds(nb + b, tc, stride=SUBLANES), :] = u1_ref[b].astype(F32)

    lo = (lax.broadcasted_iota(jnp.int32, (tile, S5_BLOCK_U), 0) & nb) == 0
    a_re = a_ref[0, :, :S5_BLOCK_S]
    a_im = a_ref[0, :, S5_BLOCK_S:]
    s_re = carry_ref[:, :S5_BLOCK_S]
    s_im = carry_ref[:, S5_BLOCK_S:]

    for s in range(tc // S5_SUB):
        base = s * tile
        uf = uf_ref[base:base + tile, :]
        lhs = jnp.concatenate([jnp.where(lo, uf, 0.0), jnp.where(lo, 0.0, uf)], axis=1).astype(BF16)
        bu_ref[base:base + tile, :] = _dot(lhs, wb_ref[0])
        for p in range(S5_SUB // 2):
            r0 = base + p * 2 * SUBLANES
            r1 = r0 + SUBLANES
            m_re = a_re * s_re - a_im * s_im + bu_ref[r0:r1, :S5_BLOCK_S]
            m_im = a_re * s_im + a_im * s_re + bu_ref[r0:r1, S5_BLOCK_S:]
            s_re = a_re * m_re - a_im * m_im + bu_ref[r1:r1 + SUBLANES, :S5_BLOCK_S]
            s_im = a_re * m_im + a_im * m_re + bu_ref[r1:r1 + SUBLANES, S5_BLOCK_S:]
            s_ref[r0:r0 + 2 * SUBLANES, :S5_BLOCK_S] = jnp.concatenate([m_re, s_re], axis=0).astype(BF16)
            s_ref[r0:r0 + 2 * SUBLANES, S5_BLOCK_S:] = jnp.concatenate([m_im, s_im], axis=0).astype(BF16)
        y8 = _dot(s_ref[base:base + tile, :], wc_ref[0])
        y = jnp.where(lo, y8[:, :S5_BLOCK_U], y8[:, S5_BLOCK_U:])
        y = (y.reshape(S5_SUB, SUBLANES, S5_BLOCK_U)
             + d_ref[...] * uf.reshape(S5_SUB, SUBLANES, S5_BLOCK_U)).reshape(tile, S5_BLOCK_U)
        cdf = 0.5 * (1.0 + jnp.tanh(math.sqrt(2.0 / math.pi) * (y + 0.044715 * (y * y * y))))
        y_ref[base:base + tile, :] = y * cdf

    carry_ref[:, :S5_BLOCK_S] = s_re
    carry_ref[:, S5_BLOCK_S:] = s_im
    for j in range(2 * nb):
        o_ref[j] = y_ref[pl.ds(j, tc, stride=SUBLANES), :].astype(o_ref.dtype)


def _s5(proj3, wb, wc, a8, d8):
    bsz, seq, _ = proj3.shape
    nsb = wb.shape[0]
    tc = min(512, seq)
    rows = SUBLANES * tc
    cb = COL_U // S5_BLOCK_U
    return pl.pallas_call(
        functools.partial(_s5_kernel, tc=tc),
        grid=(nsb, seq // tc),
        in_specs=[pl.BlockSpec((bsz, tc, S5_BLOCK_U), lambda sb, i: (0, i, cb + sb)),
                  pl.BlockSpec((bsz, tc, S5_BLOCK_U), lambda sb, i: (0, i, cb + nsb + sb)),
                  pl.BlockSpec((1, 2 * S5_BLOCK_U, 2 * S5_BLOCK_S), lambda sb, i: (sb, 0, 0)),
                  pl.BlockSpec((1, 2 * S5_BLOCK_S, 2 * S5_BLOCK_U), lambda sb, i: (sb, 0, 0)),
                  pl.BlockSpec((1, SUBLANES, 2 * S5_BLOCK_S), lambda sb, i: (sb, 0, 0)),
                  pl.BlockSpec((1, SUBLANES, S5_BLOCK_U), lambda sb, i: (sb, 0, 0))],
        out_specs=pl.BlockSpec((2 * bsz, tc, S5_BLOCK_U), lambda sb, i: (0, i, sb)),
        out_shape=jax.ShapeDtypeStruct((2 * bsz, seq, nsb * S5_BLOCK_U), BF16),
        scratch_shapes=[pltpu.VMEM((rows, S5_BLOCK_U), F32),
                        pltpu.VMEM((rows, 2 * S5_BLOCK_S), F32),
                        pltpu.VMEM((rows, 2 * S5_BLOCK_S), BF16),
                        pltpu.VMEM((rows, S5_BLOCK_U), F32),
                        pltpu.VMEM((SUBLANES, 2 * S5_BLOCK_S), F32)],
        compiler_params=_params("parallel", "arbitrary"),
        name="s5",
    )(proj3, proj3, wb, wc, a8, d8)


def _glu_kernel(y0_ref, y1_ref, z_ref, w_ref, b_ref, o_ref):
    y0 = y0_ref[0]
    y1 = y1_ref[0]
    half = y0.shape[1]
    acc = _dot(y0, w_ref[:half, :]) + _dot(y1, w_ref[half:, :]) + b_ref[...]
    y = jnp.concatenate([y0, y1], axis=1).astype(F32)
    z = z_ref[...].astype(F32)
    o_ref[...] = (y * z / ((1.0 + jnp.exp(-acc)) * (1.0 + jnp.exp(-z)))).astype(o_ref.dtype)


def _glu(yv, proj, w, b, bsz, seq):
    d = w.shape[0]
    half = d // 2
    tm = min(512, seq)
    nt = seq // tm
    return pl.pallas_call(
        _glu_kernel,
        grid=(bsz, nt),
        in_specs=[pl.BlockSpec((1, tm, half), lambda b_, i: (b_, i, 0)),
                  pl.BlockSpec((1, tm, half), lambda b_, i: (bsz + b_, i, 0)),
                  pl.BlockSpec((tm, d), lambda b_, i: (b_ * nt + i, COL_ZS // D_S5)),
                  pl.BlockSpec((d, d), lambda b_, i: (0, 0)),
                  pl.BlockSpec((1, d), lambda b_, i: (0, 0))],
        out_specs=pl.BlockSpec((tm, d), lambda b_, i: (b_ * nt + i, 0)),
        out_shape=jax.ShapeDtypeStruct((bsz * seq, d), BF16),
        compiler_params=_params("parallel", "parallel"),
        name="glu",
    )(yv, yv, proj, w, b)


OUT_EPILOGUE_CHUNKS = 2


def _out_kernel(og_ref, os_ref, w_ref, x_hbm, gate_ref, lg_ref, lb_ref, o_ref, x_buf, x_sem, *, nk, tm):
    b = pl.program_id(0)
    i = pl.program_id(1)
    k = pl.program_id(2)
    half = nk // 2

    def x_copy():
        r0 = pl.multiple_of(i * tm, tm)
        return pltpu.make_async_copy(x_hbm.at[b, pl.ds(r0, tm), :], x_buf, x_sem)

    @pl.when(k == 0)
    def _():
        x_copy().start()
        o_ref[0] = _dot(og_ref[...], w_ref[...])

    @pl.when((k > 0) & (k < half))
    def _():
        o_ref[0] += _dot(og_ref[...], w_ref[...])

    @pl.when((k >= half) & (k < nk - 1))
    def _():
        o_ref[0] += _dot(os_ref[...], w_ref[...])

    @pl.when(k == nk - 1)
    def _():
        x_copy().wait()
        rows = tm // OUT_EPILOGUE_CHUNKS
        for c in range(OUT_EPILOGUE_CHUNKS):
            rs = slice(c * rows, (c + 1) * rows)
            mixed = o_ref[0, rs, :] + _dot(os_ref[rs, :], w_ref[...])
            r = DEEPNORM_ALPHA * x_buf[rs, :] + gate_ref[0] * mixed
            mu = jnp.mean(r, axis=-1, keepdims=True)
            rc = r - mu
            var = jnp.mean(rc * rc, axis=-1, keepdims=True)
            o_ref[0, rs, :] = rc * lax.rsqrt(var + NORM_EPS) * lg_ref[...] + lb_ref[...]


def _outproj(og, osb, w, x, mod3, lg, lb):
    bsz, seq, d = x.shape
    dh = og.shape[1]
    tm = min(512, seq)
    nt = seq // tm
    tk = 1024
    half = dh // tk
    nk = 2 * half
    return pl.pallas_call(
        functools.partial(_out_kernel, nk=nk, tm=tm),
        grid=(bsz, nt, nk),
        in_specs=[pl.BlockSpec((tm, tk), lambda b, i, k: (b * nt + i, jnp.minimum(k, half - 1))),
                  pl.BlockSpec((tm, tk), lambda b, i, k: (b * nt + i, jnp.maximum(k - half, 0))),
                  pl.BlockSpec((tk, d), lambda b, i, k: (k, 0)),
                  pl.BlockSpec(memory_space=pl.ANY),
                  pl.BlockSpec((1, 1, d), lambda b, i, k: (b, 0, 2)),
                  pl.BlockSpec((1, d), lambda b, i, k: (0, 0)),
                  pl.BlockSpec((1, d), lambda b, i, k: (0, 0))],
        out_specs=pl.BlockSpec((1, tm, d), lambda b, i, k: (b, i, 0)),
        out_shape=jax.ShapeDtypeStruct(x.shape, x.dtype),
        scratch_shapes=[pltpu.VMEM((tm, d), F32), pltpu.SemaphoreType.DMA(())],
        compiler_params=_params("parallel", "parallel", "arbitrary"),
        name="outproj",
    )(og, osb, w, x, mod3, lg, lb)


def _s5_layouts(ab_re, ab_im, bbt_re, bbt_im, c_re, c_im, d_skip, bsz):
    g = ab_re.shape[0]
    nsb = g // (2 * S5_BLOCK_GROUPS)
    eye = jnp.eye(S5_BLOCK_GROUPS, dtype=F32)

    def in_map(bbt):
        t = bbt.reshape(2, nsb, S5_BLOCK_GROUPS, S5_GROUP, S5_STATE)
        t = jnp.einsum('fsihp,ij->sfihjp', t, eye)
        return t.reshape(nsb, 2 * S5_BLOCK_U, S5_BLOCK_S)

    def out_map(c):
        t = c.reshape(2, nsb, S5_BLOCK_GROUPS, S5_GROUP, S5_STATE)
        t = jnp.einsum('fsjhp,ij->sipfjh', t, eye)
        return t.reshape(nsb, S5_BLOCK_S, 2 * S5_BLOCK_U)

    def rows8(v, width):
        t = v.reshape(2, nsb, 1, width).transpose(1, 0, 2, 3)
        return jnp.broadcast_to(t, (nsb, 2, bsz, width)).reshape(nsb, 2 * bsz, width)

    wb = jnp.concatenate([in_map(bbt_re), in_map(bbt_im)], axis=2).astype(BF16)
    wc = jnp.concatenate([out_map(c_re), -out_map(c_im)], axis=1).astype(BF16)
    a8 = jnp.concatenate([rows8(ab_re.reshape(-1), S5_BLOCK_S), rows8(ab_im.reshape(-1), S5_BLOCK_S)], axis=2)
    d8 = rows8(d_skip, S5_BLOCK_U)
    return wb, wc, a8, d8


def kernel(x, c, w_ada, b_ada, w_in, w_gla_gate, b_gla_gate, gla_norm_g, s5_lambda_re, s5_lambda_im, s5_log_dt, s5_b_re, s5_b_im, s5_c_re, s5_c_im, s5_d, w_glu, b_glu, w_out, ln_g, ln_b):
    bsz, seq, d = x.shape
    assert bsz * 2 == SUBLANES and w_ada.shape[0] == DEPTH and s5_d.shape[1] == D_S5
    layer = 0

    mod = _ada(c.T, w_ada[layer], b_ada[layer][None, :])
    mod3 = mod.reshape(bsz, 1, 3 * d)

    w_main, w_glr = _wprep(jnp.swapaxes(w_in, 1, 2)[layer])
    proj, g_lr = _inproj(x, mod3, w_main, w_glr)

    wg = jnp.pad(w_gla_gate[layer], ((0, LANES - GLA_GATE_RANK), (0, 0))).astype(BF16)
    o_gla = _gla(proj, g_lr, wg, b_gla_gate[layer][None, :], gla_norm_g[layer][None, :], bsz, seq)

    bt_re = s5_b_re[layer].transpose(0, 2, 1)
    bt_im = s5_b_im[layer].transpose(0, 2, 1)
    ab_re, ab_im, bbt_re, bbt_im = _s5prep(s5_lambda_re[layer], s5_lambda_im[layer], s5_log_dt[layer],
                                            bt_re, bt_im)
    wb, wc, a8, d8 = _s5_layouts(ab_re, ab_im, bbt_re, bbt_im, s5_c_re[layer], s5_c_im[layer],
                                 s5_d[layer], bsz)
    yv = _s5(proj.reshape(bsz, seq, N_PROJ), wb, wc, a8, d8)
    o_s5 = _glu(yv, proj, w_glu[layer].astype(BF16), b_glu[layer][None, :], bsz, seq)

    return _outproj(o_gla, o_s5, w_out[layer].astype(BF16), x, mod3,
                    ln_g[layer][None, :], ln_b[layer][None, :])
```

```python
import functools
import itertools
import math

import jax
import jax.numpy as jnp
from jax import lax
from jax.experimental import pallas as pl
from jax.experimental.pallas import tpu as pltpu

F32 = jnp.float32
BF16 = jnp.bfloat16

GLA_HEADS = 4
GLA_DK = 256
GLA_DV = 512
GLA_GATE_RANK = 16
GLA_GATE_TAU = 16.0
GLA_CHUNK = 64
S5_GROUP = 16
S5_STATE = 64
NORM_EPS = 1e-5
DEPTH = 1
DEEPNORM_ALPHA = (2.0 * DEPTH) ** 0.25

LANES = 128
SUBLANES = 8
VMEM_LIMIT = 60 * 1024 * 1024

DK_TOT = GLA_HEADS * GLA_DK
D_GLA = GLA_HEADS * GLA_DV
D_S5 = 2048
COL_Q = 0
COL_K = COL_Q + DK_TOT
COL_V = COL_K + DK_TOT
COL_ZG = COL_V + D_GLA
COL_U = COL_ZG + D_GLA
COL_ZS = COL_U + D_S5
N_PROJ = COL_ZS + D_S5
MXU_WIDTH = 256
INPROJ_TN = 5 * MXU_WIDTH

S5_BLOCK_GROUPS = 8
S5_BLOCK_U = S5_BLOCK_GROUPS * S5_GROUP
S5_BLOCK_S = S5_BLOCK_GROUPS * S5_STATE
S5_SUB = 16


def _sigmoid(v):
    return 1.0 / (1.0 + jnp.exp(-v))


def _silu(v):
    return v * _sigmoid(v)


def _dot(a, b):
    return jnp.dot(a, b, preferred_element_type=F32)


def _dot_nt(a, b):
    return lax.dot_general(a, b, (((1,), (1,)), ((), ())), preferred_element_type=F32)


def _dot_tn(a, b):
    return lax.dot_general(a, b, (((0,), (0,)), ((), ())), preferred_element_type=F32)


def _params(*sem):
    return pltpu.CompilerParams(dimension_semantics=sem, vmem_limit_bytes=VMEM_LIMIT)


def _ada_kernel(ct_ref, w_ref, b_ref, o_ref, sc_ref):
    nb = sc_ref.shape[0]
    d, tn = w_ref.shape

    @pl.when(pl.program_id(0) == 0)
    def _():
        sc = _silu(ct_ref[...])
        for b in range(nb):
            sc_ref[b] = jnp.broadcast_to(sc[:, b:b + 1], (d, LANES))

    def body(g, accs):
        r0 = pl.multiple_of(g * SUBLANES, SUBLANES)
        w = w_ref[pl.ds(r0, SUBLANES), :]
        return tuple(acc + w * jnp.concatenate([sc_ref[b, pl.ds(r0, SUBLANES), :]] * (tn // LANES), axis=1)
                     for b, acc in enumerate(accs))

    accs = lax.fori_loop(0, d // SUBLANES, body,
                         tuple(jnp.zeros((SUBLANES, tn), F32) for _ in range(nb)), unroll=8)
    for b in range(nb):
        o_ref[b:b + 1, :] = jnp.sum(accs[b], axis=0, keepdims=True) + b_ref[...]


def _ada(ct, w, b):
    d, n = w.shape
    nb = ct.shape[1]
    tn = 1024
    return pl.pallas_call(
        _ada_kernel,
        grid=(n // tn,),
        in_specs=[pl.BlockSpec((d, nb), lambda j: (0, 0)),
                  pl.BlockSpec((d, tn), lambda j: (0, j)),
                  pl.BlockSpec((1, tn), lambda j: (0, j))],
        out_specs=pl.BlockSpec((nb, tn), lambda j: (0, j)),
        out_shape=jax.ShapeDtypeStruct((nb, n), F32),
        scratch_shapes=[pltpu.VMEM((nb, d, LANES), F32)],
        compiler_params=_params("arbitrary"),
        name="ada",
    )(ct, w, b)


WPREP_ROWS = 512


def _wprep_kernel(cur_ref, nxt_ref, o_ref, g_ref, *, n_plain):
    i = pl.program_id(0)
    keep = WPREP_ROWS - GLA_GATE_RANK

    @pl.when(i < n_plain)
    def _():
        o_ref[...] = cur_ref[...].astype(BF16)

    @pl.when(i >= n_plain)
    def _():
        o_ref[:keep] = cur_ref[GLA_GATE_RANK:].astype(BF16)
        o_ref[keep:] = nxt_ref[...].astype(BF16)

    @pl.when(i == n_plain - 1)
    def _():
        g_ref[:GLA_GATE_RANK] = nxt_ref[...].astype(BF16)
        g_ref[GLA_GATE_RANK:] = jnp.zeros((LANES - GLA_GATE_RANK, g_ref.shape[1]), BF16)


def _wprep(wt):
    n, d = wt.shape
    assert n == N_PROJ + GLA_GATE_RANK and COL_ZG % WPREP_ROWS == 0 and N_PROJ % WPREP_ROWS == 0
    per = WPREP_ROWS // GLA_GATE_RANK
    return pl.pallas_call(
        functools.partial(_wprep_kernel, n_plain=COL_ZG // WPREP_ROWS),
        grid=(N_PROJ // WPREP_ROWS,),
        in_specs=[pl.BlockSpec((WPREP_ROWS, d), lambda i: (i, 0)),
                  pl.BlockSpec((GLA_GATE_RANK, d), lambda i: ((i + 1) * per, 0))],
        out_specs=(pl.BlockSpec((WPREP_ROWS, d), lambda i: (i, 0)),
                   pl.BlockSpec((LANES, d), lambda i: (0, 0))),
        out_shape=(jax.ShapeDtypeStruct((N_PROJ, d), BF16),
                   jax.ShapeDtypeStruct((LANES, d), BF16)),
        compiler_params=_params("arbitrary"),
        name="wprep",
    )(wt, wt)


def _inproj_kernel(x_hbm, shift_ref, scale_ref, w_ref, wg_ref, o_ref, g_ref, h_ref, x_buf, x_sem, *, tm, nt, n_tiles):
    b = pl.program_id(0)
    i = pl.program_id(1)

    def x_copy(tile):
        r0 = pl.multiple_of((tile % nt) * tm, tm)
        return pltpu.make_async_copy(x_hbm.at[tile // nt, pl.ds(r0, tm), :], x_buf, x_sem)

    @pl.when(pl.program_id(2) == 0)
    def _():
        tile = b * nt + i

        @pl.when(tile == 0)
        def _():
            x_copy(tile).start()

        x_copy(tile).wait()
        h_ref[...] = (x_buf[...] * (1.0 + scale_ref[0]) + shift_ref[0]).astype(BF16)

        @pl.when(tile + 1 < n_tiles)
        def _():
            x_copy(tile + 1).start()

        g_ref[...] = _dot_nt(h_ref[...], wg_ref[...]).astype(g_ref.dtype)

    o_ref[...] = _dot_nt(h_ref[...], w_ref[...]).astype(o_ref.dtype)


def _inproj(x, mod3, w, wg):
    bsz, seq, d = x.shape
    n = w.shape[0]
    tn = INPROJ_TN
    tm = min(1024, seq)
    nt = seq // tm
    return pl.pallas_call(
        functools.partial(_inproj_kernel, tm=tm, nt=nt, n_tiles=bsz * nt),
        grid=(bsz, nt, n // tn),
        in_specs=[pl.BlockSpec(memory_space=pl.ANY),
                  pl.BlockSpec((1, 1, d), lambda b, i, j: (b, 0, 0)),
                  pl.BlockSpec((1, 1, d), lambda b, i, j: (b, 0, 1)),
                  pl.BlockSpec((tn, d), lambda b, i, j: (j, 0)),
                  pl.BlockSpec((LANES, d), lambda b, i, j: (0, 0))],
        out_specs=(pl.BlockSpec((tm, tn), lambda b, i, j: (b * nt + i, j)),
                   pl.BlockSpec((tm, LANES), lambda b, i, j: (b * nt + i, 0))),
        out_shape=(jax.ShapeDtypeStruct((bsz * seq, n), BF16),
                   jax.ShapeDtypeStruct((bsz * seq, LANES), BF16)),
        scratch_shapes=[pltpu.VMEM((tm, d), BF16), pltpu.VMEM((tm, d), F32), pltpu.SemaphoreType.DMA(())],
        compiler_params=_params("arbitrary", "arbitrary", "arbitrary"),
        name="inproj",
    )(x, mod3, mod3, w, wg)


GLA_BLOCK = 4 * GLA_CHUNK
GLA_NCHUNK = GLA_BLOCK // GLA_CHUNK
GLA_STEP_BLOCKS = 2


def _gla_constants():
    import numpy as np
    tc = GLA_BLOCK
    row = np.arange(tc)[:, None]
    col = np.arange(tc)[None, :]
    rc = row // GLA_CHUNK
    cc = col // GLA_CHUNK
    half = GLA_NCHUNK // 2
    same = rc == cc
    totals = np.arange(2 * SUBLANES)[:, None] == cc
    sums = np.concatenate([same & (col <= row), totals], axis=0)
    sel = np.where(same & (col <= row), 1.0,
                   np.where((rc == cc + 1) & (rc != half), 2.0,
                            np.where((rc >= half) & (cc < half), 3.0, 0.0)))
    return jnp.asarray(sums.astype(np.float32), dtype=BF16), jnp.asarray(sel.astype(np.float32))


def _gla_kernel(q_ref, k_ref, g_ref, v_ref, z_ref, wg_ref, bg_ref, ng_ref, sums_ref, sel_ref, o_ref, st_ref):
    @pl.when(pl.program_id(1) == 0)
    def _():
        st_ref[...] = jnp.zeros_like(st_ref)

    tc = GLA_BLOCK
    sel = sel_ref[...]
    logit = _dot(g_ref[...], wg_ref[...]) + bg_ref[...]
    la_all = (jnp.minimum(logit, 0.0) - jnp.log(1.0 + jnp.exp(-jnp.abs(logit)))) * (1.0 / GLA_GATE_TAU)
    for blk, hd in itertools.product(range(GLA_STEP_BLOCKS), range(GLA_HEADS)):
        rs = slice(blk * tc, (blk + 1) * tc)
        ck = slice(hd * GLA_DK, (hd + 1) * GLA_DK)
        cv = slice(hd * GLA_DV, (hd + 1) * GLA_DV)
        la = la_all[rs, ck]
        la_hi = la.astype(BF16)
        la_lo = (la - la_hi.astype(F32)).astype(BF16)
        sums = _dot(sums_ref[...], la_hi) + _dot(sums_ref[...], la_lo)
        b = sums[:tc]
        tot = [sums[tc + c:tc + c + 1] for c in range(GLA_NCHUNK)]
        half = GLA_NCHUNK // 2

        def span(lo, hi):
            return sum(tot[lo:hi]) if hi > lo else jnp.zeros_like(tot[0])

        k = k_ref[rs, ck].astype(F32)
        q_dec = q_ref[rs, ck].astype(F32) * (GLA_DK ** -0.5) * jnp.exp(b)
        k_inv = k * jnp.exp(-b)
        k_end, q_in, k_st, q_mid, k_mid = [], [], [], [], []
        for c in range(GLA_NCHUNK):
            rows = slice(c * GLA_CHUNK, (c + 1) * GLA_CHUNK)
            ke = k[rows] * jnp.exp(tot[c] - b[rows])
            e_mid = jnp.exp(span(half, c) if c >= half else span(c + 1, half))
            k_end.append(ke)
            q_in.append(q_dec[rows] * jnp.exp(span(0, c)))
            k_st.append(ke * jnp.exp(span(c + 1, GLA_NCHUNK)))
            q_mid.append(q_dec[rows] * e_mid)
            k_mid.append(ke * e_mid)
        cat = lambda parts: jnp.concatenate(parts, axis=0).astype(BF16)
        a_same = _dot_nt(q_dec.astype(BF16), k_inv.astype(BF16))
        a_next = _dot_nt(q_dec.astype(BF16), cat(k_end))
        a_mid = _dot_nt(cat(q_mid), cat(k_mid))
        att = jnp.where(sel == 1.0, a_same,
                        jnp.where(sel == 2.0, a_next, jnp.where(sel == 3.0, a_mid, 0.0))).astype(BF16)
        v = v_ref[rs, cv]
        st = st_ref[hd]
        o = _dot(att, v) + _dot_nt(cat(q_in), st.astype(BF16))
        st_ref[hd] = st * jnp.exp(span(0, GLA_NCHUNK)) + _dot_tn(v, cat(k_st))
        o = o * lax.rsqrt(jnp.mean(o * o, axis=-1, keepdims=True) + NORM_EPS) * ng_ref[...]
        z = z_ref[rs, cv].astype(F32)
        o_ref[rs, cv] = (o * _silu(z)).astype(o_ref.dtype)


def _gla(proj, g_lr, wg, bg, ng, bsz, seq):
    tc = GLA_BLOCK * GLA_STEP_BLOCKS
    nt = seq // tc
    row = lambda b, i: b * nt + i
    sums, sel = _gla_constants()
    const = lambda b, i: (0, 0)
    return pl.pallas_call(
        _gla_kernel,
        grid=(bsz, nt),
        in_specs=[pl.BlockSpec((tc, DK_TOT), lambda b, i: (row(b, i), COL_Q // DK_TOT)),
                  pl.BlockSpec((tc, DK_TOT), lambda b, i: (row(b, i), COL_K // DK_TOT)),
                  pl.BlockSpec((tc, LANES), lambda b, i: (row(b, i), 0)),
                  pl.BlockSpec((tc, D_GLA), lambda b, i: (row(b, i), COL_V // D_GLA)),
                  pl.BlockSpec((tc, D_GLA), lambda b, i: (row(b, i), COL_ZG // D_GLA)),
                  pl.BlockSpec((LANES, DK_TOT), const),
                  pl.BlockSpec((1, DK_TOT), const),
                  pl.BlockSpec((1, GLA_DV), const),
                  pl.BlockSpec((GLA_BLOCK + 2 * SUBLANES, GLA_BLOCK), const),
                  pl.BlockSpec((GLA_BLOCK, GLA_BLOCK), const)],
        out_specs=pl.BlockSpec((tc, D_GLA), lambda b, i: (row(b, i), 0)),
        out_shape=jax.ShapeDtypeStruct((bsz * seq, D_GLA), BF16),
        scratch_shapes=[pltpu.VMEM((GLA_HEADS, GLA_DV, GLA_DK), F32)],
        compiler_params=_params("parallel", "arbitrary"),
        name="gla",
    )(proj, proj, g_lr, proj, proj, wg, bg, ng, sums, sel)


def _s5prep_kernel(lre_ref, lim_ref, ldt_ref, bre_ref, bim_ref, are_ref, aim_ref, bbre_ref, bbim_ref):
    lre = lre_ref[...]
    lim = lim_ref[...]
    dt = jnp.exp(ldt_ref[...])
    z_re = lre * dt
    z_im = lim * dt
    mag = jnp.exp(z_re)
    ab_re = mag * jnp.cos(z_im)
    ab_im = mag * jnp.sin(z_im)
    den = lre * lre + lim * lim
    n_re = ab_re - 1.0
    n_im = ab_im
    f_re = (n_re * lre + n_im * lim) / den
    f_im = (n_im * lre - n_re * lim) / den
    b_re = bre_ref[...]
    b_im = bim_ref[...]
    are_ref[...] = ab_re
    aim_ref[...] = ab_im
    bbre_ref[...] = f_re * b_re - f_im * b_im
    bbim_ref[...] = f_re * b_im + f_im * b_re


def _s5prep(lam_re, lam_im, log_dt, bt_re, bt_im):
    g, p = lam_re.shape
    hh = bt_re.shape[1]
    a_shape = jax.ShapeDtypeStruct((g, 1, p), F32)
    b_shape = jax.ShapeDtypeStruct((g, hh, p), F32)
    return pl.pallas_call(
        _s5prep_kernel,
        out_shape=(a_shape, a_shape, b_shape, b_shape),
        name="s5prep",
    )(lam_re.reshape(g, 1, p), lam_im.reshape(g, 1, p), log_dt.reshape(g, 1, 1), bt_re, bt_im)


def _s5_kernel(u0_ref, u1_ref, wb_ref, wc_ref, a_ref, d_ref, o_ref, uf_ref, bu_ref, s_ref, y_ref, carry_ref, *, tc):
    @pl.when(pl.program_id(1) == 0)
    def _():
        carry_ref[...] = jnp.zeros_like(carry_ref)

    nb = u0_ref.shape[0]
    tile = SUBLANES * S5_SUB
    for b in range(nb):
        uf_ref[pl.ds(b, tc, stride=SUBLANES), :] = u0_ref[b].astype(F32)
        uf_ref[pl.ds(nb + b, tc, stride=SUBLANES), :] = u1_ref[b].astype(F32)

    lo = (lax.broadcasted_iota(jnp.int32, (tile, S5_BLOCK_U), 0) & nb) == 0
    a_re = a_ref[0, :, :S5_BLOCK_S]
    a_im = a_ref[0, :, S5_BLOCK_S:]
    s_re = carry_ref[:, :S5_BLOCK_S]
    s_im = carry_ref[:, S5_BLOCK_S:]

    for s in range(tc // S5_SUB):
        base = s * tile
        uf = uf_ref[base:base + tile, :]
        lhs = jnp.concatenate([jnp.where(lo, uf, 0.0), jnp.where(lo, 0.0, uf)], axis=1).astype(BF16)
        bu_ref[base:base + tile, :] = _dot(lhs, wb_ref[0])
        for p in range(S5_SUB // 2):
            r0 = base + p * 2 * SUBLANES
            r1 = r0 + SUBLANES
            m_re = a_re * s_re - a_im * s_im + bu_ref[r0:r1, :S5_BLOCK_S]
            m_im = a_re * s_im + a_im * s_re + bu_ref[r0:r1, S5_BLOCK_S:]
            s_re = a_re * m_re - a_im * m_im + bu_ref[r1:r1 + SUBLANES, :S5_BLOCK_S]
            s_im = a_re * m_im + a_im * m_re + bu_ref[r1:r1 + SUBLANES, S5_BLOCK_S:]
            s_ref[r0:r0 + 2 * SUBLANES, :S5_BLOCK_S] = jnp.concatenate([m_re, s_re], axis=0).astype(BF16)
            s_ref[r0:r0 + 2 * SUBLANES, S5_BLOCK_S:] = jnp.concatenate([m_im, s_im], axis=0).astype(BF16)
        y8 = _dot(s_ref[base:base + tile, :], wc_ref[0])
        y = jnp.where(lo, y8[:, :S5_BLOCK_U], y8[:, S5_BLOCK_U:])
        y = (y.reshape(S5_SUB, SUBLANES, S5_BLOCK_U)
             + d_ref[...] * uf.reshape(S5_SUB, SUBLANES, S5_BLOCK_U)).reshape(tile, S5_BLOCK_U)
        cdf = 0.5 * (1.0 + jnp.tanh(math.sqrt(2.0 / math.pi) * (y + 0.044715 * (y * y * y))))
        y_ref[base:base + tile, :] = y * cdf

    carry_ref[:, :S5_BLOCK_S] = s_re
    carry_ref[:, S5_BLOCK_S:] = s_im
    for j in range(2 * nb):
        o_ref[j] = y_ref[pl.ds(j, tc, stride=SUBLANES), :].astype(o_ref.dtype)


def _s5(proj3, wb, wc, a8, d8):
    bsz, seq, _ = proj3.shape
    nsb = wb.shape[0]
    tc = min(512, seq)
    rows = SUBLANES * tc
    cb = COL_U // S5_BLOCK_U
    return pl.pallas_call(
        functools.partial(_s5_kernel, tc=tc),
        grid=(nsb, seq // tc),
        in_specs=[pl.BlockSpec((bsz, tc, S5_BLOCK_U), lambda sb, i: (0, i, cb + sb)),
                  pl.BlockSpec((bsz, tc, S5_BLOCK_U), lambda sb, i: (0, i, cb + nsb + sb)),
                  pl.BlockSpec((1, 2 * S5_BLOCK_U, 2 * S5_BLOCK_S), lambda sb, i: (sb, 0, 0)),
                  pl.BlockSpec((1, 2 * S5_BLOCK_S, 2 * S5_BLOCK_U), lambda sb, i: (sb, 0, 0)),
                  pl.BlockSpec((1, SUBLANES, 2 * S5_BLOCK_S), lambda sb, i: (sb, 0, 0)),
                  pl.BlockSpec((1, SUBLANES, S5_BLOCK_U), lambda sb, i: (sb, 0, 0))],
        out_specs=pl.BlockSpec((2 * bsz, tc, S5_BLOCK_U), lambda sb, i: (0, i, sb)),
        out_shape=jax.ShapeDtypeStruct((2 * bsz, seq, nsb * S5_BLOCK_U), BF16),
        scratch_shapes=[pltpu.VMEM((rows, S5_BLOCK_U), F32),
                        pltpu.VMEM((rows, 2 * S5_BLOCK_S), F32),
                        pltpu.VMEM((rows, 2 * S5_BLOCK_S), BF16),
                        pltpu.VMEM((rows, S5_BLOCK_U), F32),
                        pltpu.VMEM((SUBLANES, 2 * S5_BLOCK_S), F32)],
        compiler_params=_params("parallel", "arbitrary"),
        name="s5",
    )(proj3, proj3, wb, wc, a8, d8)


def _glu_kernel(y0_ref, y1_ref, z_ref, w_ref, b_ref, o_ref):
    y0 = y0_ref[0]
    y1 = y1_ref[0]
    half = y0.shape[1]
    acc = _dot(y0, w_ref[:half, :]) + _dot(y1, w_ref[half:, :]) + b_ref[...]
    y = jnp.concatenate([y0, y1], axis=1).astype(F32)
    z = z_ref[...].astype(F32)
    o_ref[...] = (y * z / ((1.0 + jnp.exp(-acc)) * (1.0 + jnp.exp(-z)))).astype(o_ref.dtype)


def _glu(yv, proj, w, b, bsz, seq):
    d = w.shape[0]
    half = d // 2
    tm = min(512, seq)
    nt = seq // tm
    return pl.pallas_call(
        _glu_kernel,
        grid=(bsz, nt),
        in_specs=[pl.BlockSpec((1, tm, half), lambda b_, i: (b_, i, 0)),
                  pl.BlockSpec((1, tm, half), lambda b_, i: (bsz + b_, i, 0)),
                  pl.BlockSpec((tm, d), lambda b_, i: (b_ * nt + i, COL_ZS // D_S5)),
                  pl.BlockSpec((d, d), lambda b_, i: (0, 0)),
                  pl.BlockSpec((1, d), lambda b_, i: (0, 0))],
        out_specs=pl.BlockSpec((tm, d), lambda b_, i: (b_ * nt + i, 0)),
        out_shape=jax.ShapeDtypeStruct((bsz * seq, d), BF16),
        compiler_params=_params("parallel", "parallel"),
        name="glu",
    )(yv, yv, proj, w, b)


OUT_EPILOGUE_CHUNKS = 2


def _out_kernel(og_ref, os_ref, w_ref, x_hbm, gate_ref, lg_ref, lb_ref, o_ref, x_buf, x_sem, *, nk, tm):
    b = pl.program_id(0)
    i = pl.program_id(1)
    k = pl.program_id(2)
    half = nk // 2

    def x_copy():
        r0 = pl.multiple_of(i * tm, tm)
        return pltpu.make_async_copy(x_hbm.at[b, pl.ds(r0, tm), :], x_buf, x_sem)

    @pl.when(k == min(1, nk - 1))
    def _():
        x_copy().start()

    @pl.when(k == 0)
    def _():
        o_ref[0] = _dot(og_ref[...], w_ref[...])

    @pl.when((k > 0) & (k < half))
    def _():
        o_ref[0] += _dot(og_ref[...], w_ref[...])

    @pl.when((k >= half) & (k < nk - 1))
    def _():
        o_ref[0] += _dot(os_ref[...], w_ref[...])

    @pl.when(k == nk - 1)
    def _():
        x_copy().wait()
        rows = tm // OUT_EPILOGUE_CHUNKS
        for c in range(OUT_EPILOGUE_CHUNKS):
            rs = slice(c * rows, (c + 1) * rows)
            mixed = o_ref[0, rs, :] + _dot(os_ref[rs, :], w_ref[...])
            r = DEEPNORM_ALPHA * x_buf[rs, :] + gate_ref[0] * mixed
            mu = jnp.mean(r, axis=-1, keepdims=True)
            rc = r - mu
            var = jnp.mean(rc * rc, axis=-1, keepdims=True)
            o_ref[0, rs, :] = rc * lax.rsqrt(var + NORM_EPS) * lg_ref[...] + lb_ref[...]


def _outproj(og, osb, w, x, mod3, lg, lb):
    bsz, seq, d = x.shape
    dh = og.shape[1]
    tm = min(512, seq)
    nt = seq // tm
    tk = 1024
    half = dh // tk
    nk = 2 * half
    return pl.pallas_call(
        functools.partial(_out_kernel, nk=nk, tm=tm),
        grid=(bsz, nt, nk),
        in_specs=[pl.BlockSpec((tm, tk), lambda b, i, k: (b * nt + i, jnp.minimum(k, half - 1))),
                  pl.BlockSpec((tm, tk), lambda b, i, k: (b * nt + i, jnp.maximum(k - half, 0))),
                  pl.BlockSpec((tk, d), lambda b, i, k: (k, 0)),
                  pl.BlockSpec(memory_space=pl.ANY),
                  pl.BlockSpec((1, 1, d), lambda b, i, k: (b, 0, 2)),
                  pl.BlockSpec((1, d), lambda b, i, k: (0, 0)),
                  pl.BlockSpec((1, d), lambda b, i, k: (0, 0))],
        out_specs=pl.BlockSpec((1, tm, d), lambda b, i, k: (b, i, 0)),
        out_shape=jax.ShapeDtypeStruct(x.shape, x.dtype),
        scratch_shapes=[pltpu.VMEM((tm, d), F32), pltpu.SemaphoreType.DMA(())],
        compiler_params=_params("parallel", "parallel", "arbitrary"),
        name="outproj",
    )(og, osb, w, x, mod3, lg, lb)


def _s5_layouts(ab_re, ab_im, bbt_re, bbt_im, c_re, c_im, d_skip, bsz):
    g = ab_re.shape[0]
    nsb = g // (2 * S5_BLOCK_GROUPS)
    eye = jnp.eye(S5_BLOCK_GROUPS, dtype=F32)

    def in_map(bbt):
        t = bbt.reshape(2, nsb, S5_BLOCK_GROUPS, S5_GROUP, S5_STATE)
        t = jnp.einsum('fsihp,ij->sfihjp', t, eye)
        return t.reshape(nsb, 2 * S5_BLOCK_U, S5_BLOCK_S)

    def out_map(c):
        t = c.reshape(2, nsb, S5_BLOCK_GROUPS, S5_GROUP, S5_STATE)
        t = jnp.einsum('fsjhp,ij->sipfjh', t, eye)
        return t.reshape(nsb, S5_BLOCK_S, 2 * S5_BLOCK_U)

    def rows8(v, width):
        t = v.reshape(2, nsb, 1, width).transpose(1, 0, 2, 3)
        return jnp.broadcast_to(t, (nsb, 2, bsz, width)).reshape(nsb, 2 * bsz, width)

    wb = jnp.concatenate([in_map(bbt_re), in_map(bbt_im)], axis=2).astype(BF16)
    wc = jnp.concatenate([out_map(c_re), -out_map(c_im)], axis=1).astype(BF16)
    a8 = jnp.concatenate([rows8(ab_re.reshape(-1), S5_BLOCK_S), rows8(ab_im.reshape(-1), S5_BLOCK_S)], axis=2)
    d8 = rows8(d_skip, S5_BLOCK_U)
    return wb, wc, a8, d8


def kernel(x, c, w_ada, b_ada, w_in, w_gla_gate, b_gla_gate, gla_norm_g, s5_lambda_re, s5_lambda_im, s5_log_dt, s5_b_re, s5_b_im, s5_c_re, s5_c_im, s5_d, w_glu, b_glu, w_out, ln_g, ln_b):
    bsz, seq, d = x.shape
    assert bsz * 2 == SUBLANES and w_ada.shape[0] == DEPTH and s5_d.shape[1] == D_S5
    layer = 0

    mod = _ada(c.T, w_ada[layer], b_ada[layer][None, :])
    mod3 = mod.reshape(bsz, 1, 3 * d)

    w_main, w_glr = _wprep(jnp.swapaxes(w_in, 1, 2)[layer])
    proj, g_lr = _inproj(x, mod3, w_main, w_glr)

    wg = jnp.pad(w_gla_gate[layer], ((0, LANES - GLA_GATE_RANK), (0, 0))).astype(BF16)
    o_gla = _gla(proj, g_lr, wg, b_gla_gate[layer][None, :], gla_norm_g[layer][None, :], bsz, seq)

    bt_re = s5_b_re[layer].transpose(0, 2, 1)
    bt_im = s5_b_im[layer].transpose(0, 2, 1)
    ab_re, ab_im, bbt_re, bbt_im = _s5prep(s5_lambda_re[layer], s5_lambda_im[layer], s5_log_dt[layer],
                                            bt_re, bt_im)
    wb, wc, a8, d8 = _s5_layouts(ab_re, ab_im, bbt_re, bbt_im, s5_c_re[layer], s5_c_im[layer],
                                 s5_d[layer], bsz)
    yv = _s5(proj.reshape(bsz, seq, N_PROJ), wb, wc, a8, d8)
    o_s5 = _glu(yv, proj, w_glu[layer].astype(BF16), b_glu[layer][None, :], bsz, seq)

    return _outproj(o_gla, o_s5, w_out[layer].astype(BF16), x, mod3,
                    ln_g[layer][None, :], ln_b[layer][None, :])
```

```python
import functools
import itertools
import math

import jax
import jax.numpy as jnp
from jax import lax
from jax.experimental import pallas as pl
from jax.experimental.pallas import tpu as pltpu

F32 = jnp.float32
BF16 = jnp.bfloat16

GLA_HEADS = 4
GLA_DK = 256
GLA_DV = 512
GLA_GATE_RANK = 16
GLA_GATE_TAU = 16.0
GLA_CHUNK = 64
S5_GROUP = 16
S5_STATE = 64
NORM_EPS = 1e-5
DEPTH = 1
DEEPNORM_ALPHA = (2.0 * DEPTH) ** 0.25

LANES = 128
SUBLANES = 8
VMEM_LIMIT = 60 * 1024 * 1024

DK_TOT = GLA_HEADS * GLA_DK
D_GLA = GLA_HEADS * GLA_DV
D_S5 = 2048
COL_Q = 0
COL_K = COL_Q + DK_TOT
COL_V = COL_K + DK_TOT
COL_ZG = COL_V + D_GLA
COL_U = COL_ZG + D_GLA
COL_ZS = COL_U + D_S5
N_PROJ = COL_ZS + D_S5
MXU_WIDTH = 256
INPROJ_TN = 5 * MXU_WIDTH

S5_BLOCK_GROUPS = 8
S5_BLOCK_U = S5_BLOCK_GROUPS * S5_GROUP
S5_BLOCK_S = S5_BLOCK_GROUPS * S5_STATE
S5_SUB = 16


def _sigmoid(v):
    return 1.0 / (1.0 + jnp.exp(-v))


def _silu(v):
    return v * _sigmoid(v)


def _dot(a, b):
    return jnp.dot(a, b, preferred_element_type=F32)


def _dot_nt(a, b):
    return lax.dot_general(a, b, (((1,), (1,)), ((), ())), preferred_element_type=F32)


def _dot_tn(a, b):
    return lax.dot_general(a, b, (((0,), (0,)), ((), ())), preferred_element_type=F32)


def _params(*sem):
    return pltpu.CompilerParams(dimension_semantics=sem, vmem_limit_bytes=VMEM_LIMIT)


def _ada_kernel(ct_ref, w_ref, b_ref, o_ref, sc_ref):
    nb = sc_ref.shape[0]
    d, tn = w_ref.shape

    @pl.when(pl.program_id(0) == 0)
    def _():
        sc = _silu(ct_ref[...])
        for b in range(nb):
            sc_ref[b] = jnp.broadcast_to(sc[:, b:b + 1], (d, LANES))

    def body(g, accs):
        r0 = pl.multiple_of(g * SUBLANES, SUBLANES)
        w = w_ref[pl.ds(r0, SUBLANES), :]
        return tuple(acc + w * jnp.concatenate([sc_ref[b, pl.ds(r0, SUBLANES), :]] * (tn // LANES), axis=1)
                     for b, acc in enumerate(accs))

    accs = lax.fori_loop(0, d // SUBLANES, body,
                         tuple(jnp.zeros((SUBLANES, tn), F32) for _ in range(nb)), unroll=8)
    for b in range(nb):
        o_ref[b:b + 1, :] = jnp.sum(accs[b], axis=0, keepdims=True) + b_ref[...]


def _ada(ct, w, b):
    d, n = w.shape
    nb = ct.shape[1]
    tn = 1024
    return pl.pallas_call(
        _ada_kernel,
        grid=(n // tn,),
        in_specs=[pl.BlockSpec((d, nb), lambda j: (0, 0)),
                  pl.BlockSpec((d, tn), lambda j: (0, j)),
                  pl.BlockSpec((1, tn), lambda j: (0, j))],
        out_specs=pl.BlockSpec((nb, tn), lambda j: (0, j)),
        out_shape=jax.ShapeDtypeStruct((nb, n), F32),
        scratch_shapes=[pltpu.VMEM((nb, d, LANES), F32)],
        compiler_params=_params("arbitrary"),
        name="ada",
    )(ct, w, b)


WPREP_ROWS = 512


def _wprep_kernel(cur_ref, nxt_ref, o_ref, g_ref, *, n_plain):
    i = pl.program_id(0)
    keep = WPREP_ROWS - GLA_GATE_RANK

    @pl.when(i < n_plain)
    def _():
        o_ref[...] = cur_ref[...].astype(BF16)

    @pl.when(i >= n_plain)
    def _():
        o_ref[:keep] = cur_ref[GLA_GATE_RANK:].astype(BF16)
        o_ref[keep:] = nxt_ref[...].astype(BF16)

    @pl.when(i == n_plain - 1)
    def _():
        g_ref[:GLA_GATE_RANK] = nxt_ref[...].astype(BF16)
        g_ref[GLA_GATE_RANK:] = jnp.zeros((LANES - GLA_GATE_RANK, g_ref.shape[1]), BF16)


def _wprep(wt):
    n, d = wt.shape
    assert n == N_PROJ + GLA_GATE_RANK and COL_ZG % WPREP_ROWS == 0 and N_PROJ % WPREP_ROWS == 0
    per = WPREP_ROWS // GLA_GATE_RANK
    return pl.pallas_call(
        functools.partial(_wprep_kernel, n_plain=COL_ZG // WPREP_ROWS),
        grid=(N_PROJ // WPREP_ROWS,),
        in_specs=[pl.BlockSpec((WPREP_ROWS, d), lambda i: (i, 0)),
                  pl.BlockSpec((GLA_GATE_RANK, d), lambda i: ((i + 1) * per, 0))],
        out_specs=(pl.BlockSpec((WPREP_ROWS, d), lambda i: (i, 0)),
                   pl.BlockSpec((LANES, d), lambda i: (0, 0))),
        out_shape=(jax.ShapeDtypeStruct((N_PROJ, d), BF16),
                   jax.ShapeDtypeStruct((LANES, d), BF16)),
        compiler_params=_params("arbitrary"),
        name="wprep",
    )(wt, wt)


def _inproj_kernel(x_hbm, shift_ref, scale_ref, w_ref, wg_ref, o_ref, g_ref, h_ref, x_buf, x_sem, *, tm, nt, n_tiles):
    b = pl.program_id(0)
    i = pl.program_id(1)

    def x_copy(tile):
        r0 = pl.multiple_of((tile % nt) * tm, tm)
        return pltpu.make_async_copy(x_hbm.at[tile // nt, pl.ds(r0, tm), :], x_buf, x_sem)

    @pl.when(pl.program_id(2) == 0)
    def _():
        tile = b * nt + i

        @pl.when(tile == 0)
        def _():
            x_copy(tile).start()

        x_copy(tile).wait()
        h_ref[...] = (x_buf[...] * (1.0 + scale_ref[0]) + shift_ref[0]).astype(BF16)

        @pl.when(tile + 1 < n_tiles)
        def _():
            x_copy(tile + 1).start()

        g_ref[...] = _dot_nt(h_ref[...], wg_ref[...]).astype(g_ref.dtype)

    o_ref[...] = _dot_nt(h_ref[...], w_ref[...]).astype(o_ref.dtype)


def _inproj(x, mod3, w, wg):
    bsz, seq, d = x.shape
    n = w.shape[0]
    tn = INPROJ_TN
    tm = min(1024, seq)
    nt = seq // tm
    return pl.pallas_call(
        functools.partial(_inproj_kernel, tm=tm, nt=nt, n_tiles=bsz * nt),
        grid=(bsz, nt, n // tn),
        in_specs=[pl.BlockSpec(memory_space=pl.ANY),
                  pl.BlockSpec((1, 1, d), lambda b, i, j: (b, 0, 0)),
                  pl.BlockSpec((1, 1, d), lambda b, i, j: (b, 0, 1)),
                  pl.BlockSpec((tn, d), lambda b, i, j: (j, 0)),
                  pl.BlockSpec((LANES, d), lambda b, i, j: (0, 0))],
        out_specs=(pl.BlockSpec((tm, tn), lambda b, i, j: (b * nt + i, j)),
                   pl.BlockSpec((tm, LANES), lambda b, i, j: (b * nt + i, 0))),
        out_shape=(jax.ShapeDtypeStruct((bsz * seq, n), BF16),
                   jax.ShapeDtypeStruct((bsz * seq, LANES), BF16)),
        scratch_shapes=[pltpu.VMEM((tm, d), BF16), pltpu.VMEM((tm, d), F32), pltpu.SemaphoreType.DMA(())],
        compiler_params=_params("arbitrary", "arbitrary", "arbitrary"),
        name="inproj",
    )(x, mod3, mod3, w, wg)


GLA_BLOCK = 4 * GLA_CHUNK
GLA_NCHUNK = GLA_BLOCK // GLA_CHUNK
GLA_STEP_BLOCKS = 4


def _gla_constants():
    import numpy as np
    tc = GLA_BLOCK
    row = np.arange(tc)[:, None]
    col = np.arange(tc)[None, :]
    rc = row // GLA_CHUNK
    cc = col // GLA_CHUNK
    half = GLA_NCHUNK // 2
    same = rc == cc
    totals = np.arange(2 * SUBLANES)[:, None] == cc
    sums = np.concatenate([same & (col <= row), totals], axis=0)
    sel = np.where(same & (col <= row), 1.0,
                   np.where((rc == cc + 1) & (rc != half), 2.0,
                            np.where((rc >= half) & (cc < half), 3.0, 0.0)))
    return jnp.asarray(sums.astype(np.float32), dtype=BF16), jnp.asarray(sel.astype(np.float32))


def _gla_kernel(q_ref, k_ref, g_ref, v_ref, z_ref, wg_ref, bg_ref, ng_ref, sums_ref, sel_ref, o_ref, st_ref):
    @pl.when(pl.program_id(1) == 0)
    def _():
        st_ref[...] = jnp.zeros_like(st_ref)

    tc = GLA_BLOCK
    sel = sel_ref[...]
    logit = _dot(g_ref[...], wg_ref[...]) + bg_ref[...]
    la_all = (jnp.minimum(logit, 0.0) - jnp.log(1.0 + jnp.exp(-jnp.abs(logit)))) * (1.0 / GLA_GATE_TAU)
    for blk, hd in itertools.product(range(GLA_STEP_BLOCKS), range(GLA_HEADS)):
        rs = slice(blk * tc, (blk + 1) * tc)
        ck = slice(hd * GLA_DK, (hd + 1) * GLA_DK)
        cv = slice(hd * GLA_DV, (hd + 1) * GLA_DV)
        la = la_all[rs, ck]
        la_hi = la.astype(BF16)
        la_lo = (la - la_hi.astype(F32)).astype(BF16)
        sums = _dot(sums_ref[...], la_hi) + _dot(sums_ref[...], la_lo)
        b = sums[:tc]
        tot = [sums[tc + c:tc + c + 1] for c in range(GLA_NCHUNK)]
        half = GLA_NCHUNK // 2

        def span(lo, hi):
            return sum(tot[lo:hi]) if hi > lo else jnp.zeros_like(tot[0])

        k = k_ref[rs, ck].astype(F32)
        q_dec = q_ref[rs, ck].astype(F32) * (GLA_DK ** -0.5) * jnp.exp(b)
        k_inv = k * jnp.exp(-b)
        k_end, q_in, k_st, q_mid, k_mid = [], [], [], [], []
        for c in range(GLA_NCHUNK):
            rows = slice(c * GLA_CHUNK, (c + 1) * GLA_CHUNK)
            ke = k[rows] * jnp.exp(tot[c] - b[rows])
            e_mid = jnp.exp(span(half, c) if c >= half else span(c + 1, half))
            k_end.append(ke)
            q_in.append(q_dec[rows] * jnp.exp(span(0, c)))
            k_st.append(ke * jnp.exp(span(c + 1, GLA_NCHUNK)))
            q_mid.append(q_dec[rows] * e_mid)
            k_mid.append(ke * e_mid)
        cat = lambda parts: jnp.concatenate(parts, axis=0).astype(BF16)
        a_same = _dot_nt(q_dec.astype(BF16), k_inv.astype(BF16))
        a_next = _dot_nt(q_dec.astype(BF16), cat(k_end))
        a_mid = _dot_nt(cat(q_mid), cat(k_mid))
        att = jnp.where(sel == 1.0, a_same,
                        jnp.where(sel == 2.0, a_next, jnp.where(sel == 3.0, a_mid, 0.0))).astype(BF16)
        v = v_ref[rs, cv]
        st = st_ref[hd]
        o = _dot(att, v) + _dot_nt(cat(q_in), st.astype(BF16))
        st_ref[hd] = st * jnp.exp(span(0, GLA_NCHUNK)) + _dot_tn(v, cat(k_st))
        o = o * lax.rsqrt(jnp.mean(o * o, axis=-1, keepdims=True) + NORM_EPS) * ng_ref[...]
        z = z_ref[rs, cv].astype(F32)
        o_ref[rs, cv] = (o * _silu(z)).astype(o_ref.dtype)


def _gla(proj, g_lr, wg, bg, ng, bsz, seq):
    tc = GLA_BLOCK * GLA_STEP_BLOCKS
    nt = seq // tc
    row = lambda b, i: b * nt + i
    sums, sel = _gla_constants()
    const = lambda b, i: (0, 0)
    return pl.pallas_call(
        _gla_kernel,
        grid=(bsz, nt),
        in_specs=[pl.BlockSpec((tc, DK_TOT), lambda b, i: (row(b, i), COL_Q // DK_TOT)),
                  pl.BlockSpec((tc, DK_TOT), lambda b, i: (row(b, i), COL_K // DK_TOT)),
                  pl.BlockSpec((tc, LANES), lambda b, i: (row(b, i), 0)),
                  pl.BlockSpec((tc, D_GLA), lambda b, i: (row(b, i), COL_V // D_GLA)),
                  pl.BlockSpec((tc, D_GLA), lambda b, i: (row(b, i), COL_ZG // D_GLA)),
                  pl.BlockSpec((LANES, DK_TOT), const),
                  pl.BlockSpec((1, DK_TOT), const),
                  pl.BlockSpec((1, GLA_DV), const),
                  pl.BlockSpec((GLA_BLOCK + 2 * SUBLANES, GLA_BLOCK), const),
                  pl.BlockSpec((GLA_BLOCK, GLA_BLOCK), const)],
        out_specs=pl.BlockSpec((tc, D_GLA), lambda b, i: (row(b, i), 0)),
        out_shape=jax.ShapeDtypeStruct((bsz * seq, D_GLA), BF16),
        scratch_shapes=[pltpu.VMEM((GLA_HEADS, GLA_DV, GLA_DK), F32)],
        compiler_params=_params("parallel", "arbitrary"),
        name="gla",
    )(proj, proj, g_lr, proj, proj, wg, bg, ng, sums, sel)


def _s5prep_kernel(lre_ref, lim_ref, ldt_ref, bre_ref, bim_ref, are_ref, aim_ref, bbre_ref, bbim_ref):
    lre = lre_ref[...]
    lim = lim_ref[...]
    dt = jnp.exp(ldt_ref[...])
    z_re = lre * dt
    z_im = lim * dt
    mag = jnp.exp(z_re)
    ab_re = mag * jnp.cos(z_im)
    ab_im = mag * jnp.sin(z_im)
    den = lre * lre + lim * lim
    n_re = ab_re - 1.0
    n_im = ab_im
    f_re = (n_re * lre + n_im * lim) / den
    f_im = (n_im * lre - n_re * lim) / den
    b_re = bre_ref[...]
    b_im = bim_ref[...]
    are_ref[...] = ab_re
    aim_ref[...] = ab_im
    bbre_ref[...] = f_re * b_re - f_im * b_im
    bbim_ref[...] = f_re * b_im + f_im * b_re


def _s5prep(lam_re, lam_im, log_dt, bt_re, bt_im):
    g, p = lam_re.shape
    hh = bt_re.shape[1]
    a_shape = jax.ShapeDtypeStruct((g, 1, p), F32)
    b_shape = jax.ShapeDtypeStruct((g, hh, p), F32)
    return pl.pallas_call(
        _s5prep_kernel,
        out_shape=(a_shape, a_shape, b_shape, b_shape),
        name="s5prep",
    )(lam_re.reshape(g, 1, p), lam_im.reshape(g, 1, p), log_dt.reshape(g, 1, 1), bt_re, bt_im)


def _s5_kernel(u0_ref, u1_ref, wb_ref, wc_ref, a_ref, d_ref, o_ref, uf_ref, bu_ref, s_ref, y_ref, carry_ref, *, tc):
    @pl.when(pl.program_id(1) == 0)
    def _():
        carry_ref[...] = jnp.zeros_like(carry_ref)

    nb = u0_ref.shape[0]
    tile = SUBLANES * S5_SUB
    for b in range(nb):
        uf_ref[pl.ds(b, tc, stride=SUBLANES), :] = u0_ref[b].astype(F32)
        uf_ref[pl.ds(nb + b, tc, stride=SUBLANES), :] = u1_ref[b].astype(F32)

    lo = (lax.broadcasted_iota(jnp.int32, (tile, S5_BLOCK_U), 0) & nb) == 0
    a_re = a_ref[0, :, :S5_BLOCK_S]
    a_im = a_ref[0, :, S5_BLOCK_S:]
    s_re = carry_ref[:, :S5_BLOCK_S]
    s_im = carry_ref[:, S5_BLOCK_S:]

    for s in range(tc // S5_SUB):
        base = s * tile
        uf = uf_ref[base:base + tile, :]
        lhs = jnp.concatenate([jnp.where(lo, uf, 0.0), jnp.where(lo, 0.0, uf)], axis=1).astype(BF16)
        bu_ref[base:base + tile, :] = _dot(lhs, wb_ref[0])
        for p in range(S5_SUB // 2):
            r0 = base + p * 2 * SUBLANES
            r1 = r0 + SUBLANES
            m_re = a_re * s_re - a_im * s_im + bu_ref[r0:r1, :S5_BLOCK_S]
            m_im = a_re * s_im + a_im * s_re + bu_ref[r0:r1, S5_BLOCK_S:]
            s_re = a_re * m_re - a_im * m_im + bu_ref[r1:r1 + SUBLANES, :S5_BLOCK_S]
            s_im = a_re * m_im + a_im * m_re + bu_ref[r1:r1 + SUBLANES, S5_BLOCK_S:]
            s_ref[r0:r0 + 2 * SUBLANES, :S5_BLOCK_S] = jnp.concatenate([m_re, s_re], axis=0).astype(BF16)
            s_ref[r0:r0 + 2 * SUBLANES, S5_BLOCK_S:] = jnp.concatenate([m_im, s_im], axis=0).astype(BF16)
        y8 = _dot(s_ref[base:base + tile, :], wc_ref[0])
        y = jnp.where(lo, y8[:, :S5_BLOCK_U], y8[:, S5_BLOCK_U:])
        y = (y.reshape(S5_SUB, SUBLANES, S5_BLOCK_U)
             + d_ref[...] * uf.reshape(S5_SUB, SUBLANES, S5_BLOCK_U)).reshape(tile, S5_BLOCK_U)
        cdf = 0.5 * (1.0 + jnp.tanh(math.sqrt(2.0 / math.pi) * (y + 0.044715 * (y * y * y))))
        y_ref[base:base + tile, :] = y * cdf

    carry_ref[:, :S5_BLOCK_S] = s_re
    carry_ref[:, S5_BLOCK_S:] = s_im
    for j in range(2 * nb):
        o_ref[j] = y_ref[pl.ds(j, tc, stride=SUBLANES), :].astype(o_ref.dtype)


def _s5(proj3, wb, wc, a8, d8):
    bsz, seq, _ = proj3.shape
    nsb = wb.shape[0]
    tc = min(512, seq)
    rows = SUBLANES * tc
    cb = COL_U // S5_BLOCK_U
    return pl.pallas_call(
        functools.partial(_s5_kernel, tc=tc),
        grid=(nsb, seq // tc),
        in_specs=[pl.BlockSpec((bsz, tc, S5_BLOCK_U), lambda sb, i: (0, i, cb + sb)),
                  pl.BlockSpec((bsz, tc, S5_BLOCK_U), lambda sb, i: (0, i, cb + nsb + sb)),
                  pl.BlockSpec((1, 2 * S5_BLOCK_U, 2 * S5_BLOCK_S), lambda sb, i: (sb, 0, 0)),
                  pl.BlockSpec((1, 2 * S5_BLOCK_S, 2 * S5_BLOCK_U), lambda sb, i: (sb, 0, 0)),
                  pl.BlockSpec((1, SUBLANES, 2 * S5_BLOCK_S), lambda sb, i: (sb, 0, 0)),
                  pl.BlockSpec((1, SUBLANES, S5_BLOCK_U), lambda sb, i: (sb, 0, 0))],
        out_specs=pl.BlockSpec((2 * bsz, tc, S5_BLOCK_U), lambda sb, i: (0, i, sb)),
        out_shape=jax.ShapeDtypeStruct((2 * bsz, seq, nsb * S5_BLOCK_U), BF16),
        scratch_shapes=[pltpu.VMEM((rows, S5_BLOCK_U), F32),
                        pltpu.VMEM((rows, 2 * S5_BLOCK_S), F32),
                        pltpu.VMEM((rows, 2 * S5_BLOCK_S), BF16),
                        pltpu.VMEM((rows, S5_BLOCK_U), F32),
                        pltpu.VMEM((SUBLANES, 2 * S5_BLOCK_S), F32)],
        compiler_params=_params("parallel", "arbitrary"),
        name="s5",
    )(proj3, proj3, wb, wc, a8, d8)


def _glu_kernel(y0_ref, y1_ref, z_ref, w_ref, b_ref, o_ref):
    yb = jnp.concatenate([y0_ref[0], y1_ref[0]], axis=1)
    acc = _dot(yb, w_ref[...]) + b_ref[...]
    y = yb.astype(F32)
    z = z_ref[...].astype(F32)
    o_ref[...] = (y * z / ((1.0 + jnp.exp(-acc)) * (1.0 + jnp.exp(-z)))).astype(o_ref.dtype)


def _glu(yv, proj, w, b, bsz, seq):
    d = w.shape[0]
    half = d // 2
    tm = min(1024, seq)
    nt = seq // tm
    return pl.pallas_call(
        _glu_kernel,
        grid=(bsz, nt),
        in_specs=[pl.BlockSpec((1, tm, half), lambda b_, i: (b_, i, 0)),
                  pl.BlockSpec((1, tm, half), lambda b_, i: (bsz + b_, i, 0)),
                  pl.BlockSpec((tm, d), lambda b_, i: (b_ * nt + i, COL_ZS // D_S5)),
                  pl.BlockSpec((d, d), lambda b_, i: (0, 0)),
                  pl.BlockSpec((1, d), lambda b_, i: (0, 0))],
        out_specs=pl.BlockSpec((tm, d), lambda b_, i: (b_ * nt + i, 0)),
        out_shape=jax.ShapeDtypeStruct((bsz * seq, d), BF16),
        compiler_params=_params("parallel", "parallel"),
        name="glu",
    )(yv, yv, proj, w, b)


OUT_EPILOGUE_CHUNKS = 2


def _out_kernel(og_ref, os_ref, w_ref, x_hbm, gate_ref, lg_ref, lb_ref, o_ref, x_buf, x_sem, *, nk, tm):
    b = pl.program_id(0)
    i = pl.program_id(1)
    k = pl.program_id(2)
    half = nk // 2

    def x_copy():
        r0 = pl.multiple_of(i * tm, tm)
        return pltpu.make_async_copy(x_hbm.at[b, pl.ds(r0, tm), :], x_buf, x_sem)

    @pl.when(k == min(1, nk - 1))
    def _():
        x_copy().start()

    @pl.when(k == 0)
    def _():
        o_ref[0] = _dot(og_ref[...], w_ref[...])

    @pl.when((k > 0) & (k < half))
    def _():
        o_ref[0] += _dot(og_ref[...], w_ref[...])

    @pl.when((k >= half) & (k < nk - 1))
    def _():
        o_ref[0] += _dot(os_ref[...], w_ref[...])

    @pl.when(k == nk - 1)
    def _():
        x_copy().wait()
        rows = tm // OUT_EPILOGUE_CHUNKS
        for c in range(OUT_EPILOGUE_CHUNKS):
            rs = slice(c * rows, (c + 1) * rows)
            mixed = o_ref[0, rs, :] + _dot(os_ref[rs, :], w_ref[...])
            r = DEEPNORM_ALPHA * x_buf[rs, :] + gate_ref[0] * mixed
            mu = jnp.mean(r, axis=-1, keepdims=True)
            rc = r - mu
            var = jnp.mean(rc * rc, axis=-1, keepdims=True)
            o_ref[0, rs, :] = rc * lax.rsqrt(var + NORM_EPS) * lg_ref[...] + lb_ref[...]


def _outproj(og, osb, w, x, mod3, lg, lb):
    bsz, seq, d = x.shape
    dh = og.shape[1]
    tm = min(512, seq)
    nt = seq // tm
    tk = 1024
    half = dh // tk
    nk = 2 * half
    return pl.pallas_call(
        functools.partial(_out_kernel, nk=nk, tm=tm),
        grid=(bsz, nt, nk),
        in_specs=[pl.BlockSpec((tm, tk), lambda b, i, k: (b * nt + i, jnp.minimum(k, half - 1))),
                  pl.BlockSpec((tm, tk), lambda b, i, k: (b * nt + i, jnp.maximum(k - half, 0))),
                  pl.BlockSpec((tk, d), lambda b, i, k: (k, 0)),
                  pl.BlockSpec(memory_space=pl.ANY),
                  pl.BlockSpec((1, 1, d), lambda b, i, k: (b, 0, 2)),
                  pl.BlockSpec((1, d), lambda b, i, k: (0, 0)),
                  pl.BlockSpec((1, d), lambda b, i, k: (0, 0))],
        out_specs=pl.BlockSpec((1, tm, d), lambda b, i, k: (b, i, 0)),
        out_shape=jax.ShapeDtypeStruct(x.shape, x.dtype),
        scratch_shapes=[pltpu.VMEM((tm, d), F32), pltpu.SemaphoreType.DMA(())],
        compiler_params=_params("parallel", "parallel", "arbitrary"),
        name="outproj",
    )(og, osb, w, x, mod3, lg, lb)


def _s5_layouts(ab_re, ab_im, bbt_re, bbt_im, c_re, c_im, d_skip, bsz):
    g = ab_re.shape[0]
    nsb = g // (2 * S5_BLOCK_GROUPS)
    eye = jnp.eye(S5_BLOCK_GROUPS, dtype=F32)

    def in_map(bbt):
        t = bbt.reshape(2, nsb, S5_BLOCK_GROUPS, S5_GROUP, S5_STATE)
        t = jnp.einsum('fsihp,ij->sfihjp', t, eye)
        return t.reshape(nsb, 2 * S5_BLOCK_U, S5_BLOCK_S)

    def out_map(c):
        t = c.reshape(2, nsb, S5_BLOCK_GROUPS, S5_GROUP, S5_STATE)
        t = jnp.einsum('fsjhp,ij->sipfjh', t, eye)
        return t.reshape(nsb, S5_BLOCK_S, 2 * S5_BLOCK_U)

    def rows8(v, width):
        t = v.reshape(2, nsb, 1, width).transpose(1, 0, 2, 3)
        return jnp.broadcast_to(t, (nsb, 2, bsz, width)).reshape(nsb, 2 * bsz, width)

    wb = jnp.concatenate([in_map(bbt_re), in_map(bbt_im)], axis=2).astype(BF16)
    wc = jnp.concatenate([out_map(c_re), -out_map(c_im)], axis=1).astype(BF16)
    a8 = jnp.concatenate([rows8(ab_re.reshape(-1), S5_BLOCK_S), rows8(ab_im.reshape(-1), S5_BLOCK_S)], axis=2)
    d8 = rows8(d_skip, S5_BLOCK_U)
    return wb, wc, a8, d8


def kernel(x, c, w_ada, b_ada, w_in, w_gla_gate, b_gla_gate, gla_norm_g, s5_lambda_re, s5_lambda_im, s5_log_dt, s5_b_re, s5_b_im, s5_c_re, s5_c_im, s5_d, w_glu, b_glu, w_out, ln_g, ln_b):
    bsz, seq, d = x.shape
    assert bsz * 2 == SUBLANES and w_ada.shape[0] == DEPTH and s5_d.shape[1] == D_S5
    layer = 0

    mod = _ada(c.T, w_ada[layer], b_ada[layer][None, :])
    mod3 = mod.reshape(bsz, 1, 3 * d)

    w_main, w_glr = _wprep(jnp.swapaxes(w_in, 1, 2)[layer])
    proj, g_lr = _inproj(x, mod3, w_main, w_glr)

    wg = jnp.pad(w_gla_gate[layer], ((0, LANES - GLA_GATE_RANK), (0, 0))).astype(BF16)
    o_gla = _gla(proj, g_lr, wg, b_gla_gate[layer][None, :], gla_norm_g[layer][None, :], bsz, seq)

    bt_re = s5_b_re[layer].transpose(0, 2, 1)
    bt_im = s5_b_im[layer].transpose(0, 2, 1)
    ab_re, ab_im, bbt_re, bbt_im = _s5prep(s5_lambda_re[layer], s5_lambda_im[layer], s5_log_dt[layer],
                                            bt_re, bt_im)
    wb, wc, a8, d8 = _s5_layouts(ab_re, ab_im, bbt_re, bbt_im, s5_c_re[layer], s5_c_im[layer],
                                 s5_d[layer], bsz)
    yv = _s5(proj.reshape(bsz, seq, N_PROJ), wb, wc, a8, d8)
    o_s5 = _glu(yv, proj, w_glu[layer].astype(BF16), b_glu[layer][None, :], bsz, seq)

    return _outproj(o_gla, o_s5, w_out[layer].astype(BF16), x, mod3,
                    ln_g[layer][None, :], ln_b[layer][None, :])
```

```python
import functools
import itertools
import math

import jax
import jax.numpy as jnp
from jax import lax
from jax.experimental import pallas as pl
from jax.experimental.pallas import tpu as pltpu

F32 = jnp.float32
BF16 = jnp.bfloat16

GLA_HEADS = 4
GLA_DK = 256
GLA_DV = 512
GLA_GATE_RANK = 16
GLA_GATE_TAU = 16.0
GLA_CHUNK = 64
S5_GROUP = 16
S5_STATE = 64
NORM_EPS = 1e-5
DEPTH = 1
DEEPNORM_ALPHA = (2.0 * DEPTH) ** 0.25

LANES = 128
SUBLANES = 8
VMEM_LIMIT = 60 * 1024 * 1024

DK_TOT = GLA_HEADS * GLA_DK
D_GLA = GLA_HEADS * GLA_DV
D_S5 = 2048
COL_Q = 0
COL_K = COL_Q + DK_TOT
COL_V = COL_K + DK_TOT
COL_ZG = COL_V + D_GLA
COL_U = COL_ZG + D_GLA
COL_ZS = COL_U + D_S5
N_PROJ = COL_ZS + D_S5
MXU_WIDTH = 256
INPROJ_TN = 5 * MXU_WIDTH

S5_BLOCK_GROUPS = 8
S5_BLOCK_U = S5_BLOCK_GROUPS * S5_GROUP
S5_BLOCK_S = S5_BLOCK_GROUPS * S5_STATE
S5_SUB = 16


def _sigmoid(v):
    return 1.0 / (1.0 + jnp.exp(-v))


def _silu(v):
    return v * _sigmoid(v)


def _dot(a, b):
    return jnp.dot(a, b, preferred_element_type=F32)


def _dot_nt(a, b):
    return lax.dot_general(a, b, (((1,), (1,)), ((), ())), preferred_element_type=F32)


def _dot_tn(a, b):
    return lax.dot_general(a, b, (((0,), (0,)), ((), ())), preferred_element_type=F32)


def _params(*sem):
    return pltpu.CompilerParams(dimension_semantics=sem, vmem_limit_bytes=VMEM_LIMIT)


def _ada_kernel(ct_ref, w_ref, b_ref, o_ref, sc_ref):
    nb = sc_ref.shape[0]
    d, tn = w_ref.shape

    @pl.when(pl.program_id(0) == 0)
    def _():
        sc = _silu(ct_ref[...])
        for b in range(nb):
            sc_ref[b] = jnp.broadcast_to(sc[:, b:b + 1], (d, LANES))

    def body(g, accs):
        r0 = pl.multiple_of(g * SUBLANES, SUBLANES)
        w = w_ref[pl.ds(r0, SUBLANES), :]
        return tuple(acc + w * jnp.concatenate([sc_ref[b, pl.ds(r0, SUBLANES), :]] * (tn // LANES), axis=1)
                     for b, acc in enumerate(accs))

    accs = lax.fori_loop(0, d // SUBLANES, body,
                         tuple(jnp.zeros((SUBLANES, tn), F32) for _ in range(nb)), unroll=8)
    for b in range(nb):
        o_ref[b:b + 1, :] = jnp.sum(accs[b], axis=0, keepdims=True) + b_ref[...]


def _ada(ct, w, b):
    d, n = w.shape
    nb = ct.shape[1]
    tn = 1024
    return pl.pallas_call(
        _ada_kernel,
        grid=(n // tn,),
        in_specs=[pl.BlockSpec((d, nb), lambda j: (0, 0)),
                  pl.BlockSpec((d, tn), lambda j: (0, j)),
                  pl.BlockSpec((1, tn), lambda j: (0, j))],
        out_specs=pl.BlockSpec((nb, tn), lambda j: (0, j)),
        out_shape=jax.ShapeDtypeStruct((nb, n), F32),
        scratch_shapes=[pltpu.VMEM((nb, d, LANES), F32)],
        compiler_params=_params("arbitrary"),
        name="ada",
    )(ct, w, b)


WPREP_ROWS = 512


def _wprep_kernel(cur_ref, nxt_ref, o_ref, g_ref, *, n_plain):
    i = pl.program_id(0)
    keep = WPREP_ROWS - GLA_GATE_RANK

    @pl.when(i < n_plain)
    def _():
        o_ref[...] = cur_ref[...].astype(BF16)

    @pl.when(i >= n_plain)
    def _():
        o_ref[:keep] = cur_ref[GLA_GATE_RANK:].astype(BF16)
        o_ref[keep:] = nxt_ref[...].astype(BF16)

    @pl.when(i == n_plain - 1)
    def _():
        g_ref[:GLA_GATE_RANK] = nxt_ref[...].astype(BF16)
        g_ref[GLA_GATE_RANK:] = jnp.zeros((LANES - GLA_GATE_RANK, g_ref.shape[1]), BF16)


def _wprep(wt):
    n, d = wt.shape
    assert n == N_PROJ + GLA_GATE_RANK and COL_ZG % WPREP_ROWS == 0 and N_PROJ % WPREP_ROWS == 0
    per = WPREP_ROWS // GLA_GATE_RANK
    return pl.pallas_call(
        functools.partial(_wprep_kernel, n_plain=COL_ZG // WPREP_ROWS),
        grid=(N_PROJ // WPREP_ROWS,),
        in_specs=[pl.BlockSpec((WPREP_ROWS, d), lambda i: (i, 0)),
                  pl.BlockSpec((GLA_GATE_RANK, d), lambda i: ((i + 1) * per, 0))],
        out_specs=(pl.BlockSpec((WPREP_ROWS, d), lambda i: (i, 0)),
                   pl.BlockSpec((LANES, d), lambda i: (0, 0))),
        out_shape=(jax.ShapeDtypeStruct((N_PROJ, d), BF16),
                   jax.ShapeDtypeStruct((LANES, d), BF16)),
        compiler_params=_params("arbitrary"),
        name="wprep",
    )(wt, wt)


def _inproj_kernel(x_hbm, shift_ref, scale_ref, w_ref, wg_ref, o_ref, g_ref, h_ref, x_buf, x_sem, *, tm, nt, n_tiles):
    b = pl.program_id(0)
    i = pl.program_id(1)

    def x_copy(tile):
        r0 = pl.multiple_of((tile % nt) * tm, tm)
        return pltpu.make_async_copy(x_hbm.at[tile // nt, pl.ds(r0, tm), :], x_buf, x_sem)

    @pl.when(pl.program_id(2) == 0)
    def _():
        tile = b * nt + i

        @pl.when(tile == 0)
        def _():
            x_copy(tile).start()

        x_copy(tile).wait()
        h_ref[...] = (x_buf[...] * (1.0 + scale_ref[0]) + shift_ref[0]).astype(BF16)

        @pl.when(tile + 1 < n_tiles)
        def _():
            x_copy(tile + 1).start()

        g_ref[...] = _dot_nt(h_ref[...], wg_ref[...]).astype(g_ref.dtype)

    o_ref[...] = _dot_nt(h_ref[...], w_ref[...]).astype(o_ref.dtype)


def _inproj(x, mod3, w, wg):
    bsz, seq, d = x.shape
    n = w.shape[0]
    tn = INPROJ_TN
    tm = min(1024, seq)
    nt = seq // tm
    return pl.pallas_call(
        functools.partial(_inproj_kernel, tm=tm, nt=nt, n_tiles=bsz * nt),
        grid=(bsz, nt, n // tn),
        in_specs=[pl.BlockSpec(memory_space=pl.ANY),
                  pl.BlockSpec((1, 1, d), lambda b, i, j: (b, 0, 0)),
                  pl.BlockSpec((1, 1, d), lambda b, i, j: (b, 0, 1)),
                  pl.BlockSpec((tn, d), lambda b, i, j: (j, 0)),
                  pl.BlockSpec((LANES, d), lambda b, i, j: (0, 0))],
        out_specs=(pl.BlockSpec((tm, tn), lambda b, i, j: (b * nt + i, j)),
                   pl.BlockSpec((tm, LANES), lambda b, i, j: (b * nt + i, 0))),
        out_shape=(jax.ShapeDtypeStruct((bsz * seq, n), BF16),
                   jax.ShapeDtypeStruct((bsz * seq, LANES), BF16)),
        scratch_shapes=[pltpu.VMEM((tm, d), BF16), pltpu.VMEM((tm, d), F32), pltpu.SemaphoreType.DMA(())],
        compiler_params=_params("arbitrary", "arbitrary", "arbitrary"),
        name="inproj",
    )(x, mod3, mod3, w, wg)


GLA_BLOCK = 4 * GLA_CHUNK
GLA_NCHUNK = GLA_BLOCK // GLA_CHUNK
GLA_STEP_BLOCKS = 4


def _gla_constants():
    import numpy as np
    tc = GLA_BLOCK
    row = np.arange(tc)[:, None]
    col = np.arange(tc)[None, :]
    rc = row // GLA_CHUNK
    cc = col // GLA_CHUNK
    half = GLA_NCHUNK // 2
    same = rc == cc
    totals = np.arange(2 * SUBLANES)[:, None] == cc
    sums = np.concatenate([same & (col <= row), totals], axis=0)
    sel = np.where(same & (col <= row), 1.0,
                   np.where((rc == cc + 1) & (rc != half), 2.0,
                            np.where((rc >= half) & (cc < half), 3.0, 0.0)))
    return jnp.asarray(sums.astype(np.float32), dtype=BF16), jnp.asarray(sel.astype(np.float32))


def _gla_kernel(q_ref, k_ref, g_ref, v_ref, z_ref, wg_ref, bg_ref, ng_ref, sums_ref, sel_ref, o_ref, st_ref):
    @pl.when(pl.program_id(1) == 0)
    def _():
        st_ref[...] = jnp.zeros_like(st_ref)

    tc = GLA_BLOCK
    sel = sel_ref[...]
    logit = _dot(g_ref[...], wg_ref[...]) + bg_ref[...]
    la_all = (jnp.minimum(logit, 0.0) - jnp.log(1.0 + jnp.exp(-jnp.abs(logit)))) * (1.0 / GLA_GATE_TAU)
    for blk, hd in itertools.product(range(GLA_STEP_BLOCKS), range(GLA_HEADS)):
        rs = slice(blk * tc, (blk + 1) * tc)
        ck = slice(hd * GLA_DK, (hd + 1) * GLA_DK)
        cv = slice(hd * GLA_DV, (hd + 1) * GLA_DV)
        la = la_all[rs, ck]
        la_hi = la.astype(BF16)
        la_lo = (la - la_hi.astype(F32)).astype(BF16)
        sums = _dot(sums_ref[...], la_hi) + _dot(sums_ref[...], la_lo)
        b = sums[:tc]
        tot = [sums[tc + c:tc + c + 1] for c in range(GLA_NCHUNK)]
        half = GLA_NCHUNK // 2

        def span(lo, hi):
            return sum(tot[lo:hi]) if hi > lo else jnp.zeros_like(tot[0])

        k = k_ref[rs, ck].astype(F32)
        q_dec = q_ref[rs, ck].astype(F32) * (GLA_DK ** -0.5) * jnp.exp(b)
        k_inv = k * jnp.exp(-b)
        k_end, q_in, k_st, q_mid, k_mid = [], [], [], [], []
        for c in range(GLA_NCHUNK):
            rows = slice(c * GLA_CHUNK, (c + 1) * GLA_CHUNK)
            ke = k[rows] * jnp.exp(tot[c] - b[rows])
            e_mid = jnp.exp(span(half, c) if c >= half else span(c + 1, half))
            k_end.append(ke)
            q_in.append(q_dec[rows] * jnp.exp(span(0, c)))
            k_st.append(ke * jnp.exp(span(c + 1, GLA_NCHUNK)))
            q_mid.append(q_dec[rows] * e_mid)
            k_mid.append(ke * e_mid)
        cat = lambda parts: jnp.concatenate(parts, axis=0).astype(BF16)
        a_same = _dot_nt(q_dec.astype(BF16), k_inv.astype(BF16))
        a_next = _dot_nt(q_dec.astype(BF16), cat(k_end))
        a_mid = _dot_nt(cat(q_mid), cat(k_mid))
        att = jnp.where(sel == 1.0, a_same,
                        jnp.where(sel == 2.0, a_next, jnp.where(sel == 3.0, a_mid, 0.0))).astype(BF16)
        v = v_ref[rs, cv]
        st = st_ref[hd]
        o = _dot(att, v) + _dot_nt(cat(q_in), st.astype(BF16))
        st_ref[hd] = st * jnp.exp(span(0, GLA_NCHUNK)) + _dot_tn(v, cat(k_st))
        o = o * lax.rsqrt(jnp.mean(o * o, axis=-1, keepdims=True) + NORM_EPS) * ng_ref[...]
        z = z_ref[rs, cv].astype(F32)
        o_ref[rs, cv] = (o * _silu(z)).astype(o_ref.dtype)


def _gla(proj, g_lr, wg, bg, ng, bsz, seq):
    tc = GLA_BLOCK * GLA_STEP_BLOCKS
    nt = seq // tc
    row = lambda b, i: b * nt + i
    sums, sel = _gla_constants()
    const = lambda b, i: (0, 0)
    return pl.pallas_call(
        _gla_kernel,
        grid=(bsz, nt),
        in_specs=[pl.BlockSpec((tc, DK_TOT), lambda b, i: (row(b, i), COL_Q // DK_TOT)),
                  pl.BlockSpec((tc, DK_TOT), lambda b, i: (row(b, i), COL_K // DK_TOT)),
                  pl.BlockSpec((tc, LANES), lambda b, i: (row(b, i), 0)),
                  pl.BlockSpec((tc, D_GLA), lambda b, i: (row(b, i), COL_V // D_GLA)),
                  pl.BlockSpec((tc, D_GLA), lambda b, i: (row(b, i), COL_ZG // D_GLA)),
                  pl.BlockSpec((LANES, DK_TOT), const),
                  pl.BlockSpec((1, DK_TOT), const),
                  pl.BlockSpec((1, GLA_DV), const),
                  pl.BlockSpec((GLA_BLOCK + 2 * SUBLANES, GLA_BLOCK), const),
                  pl.BlockSpec((GLA_BLOCK, GLA_BLOCK), const)],
        out_specs=pl.BlockSpec((tc, D_GLA), lambda b, i: (row(b, i), 0)),
        out_shape=jax.ShapeDtypeStruct((bsz * seq, D_GLA), BF16),
        scratch_shapes=[pltpu.VMEM((GLA_HEADS, GLA_DV, GLA_DK), F32)],
        compiler_params=_params("parallel", "arbitrary"),
        name="gla",
    )(proj, proj, g_lr, proj, proj, wg, bg, ng, sums, sel)


def _s5prep_kernel(lre_ref, lim_ref, ldt_ref, bre_ref, bim_ref, cre_ref, cim_ref,
                   are_ref, aim_ref, wb_ref, wc_ref):
    lre = lre_ref[...]
    lim = lim_ref[...]
    dt = jnp.exp(ldt_ref[...])
    z_re = lre * dt
    z_im = lim * dt
    mag = jnp.exp(z_re)
    ab_re = mag * jnp.cos(z_im)
    ab_im = mag * jnp.sin(z_im)
    den = lre * lre + lim * lim
    n_re = ab_re - 1.0
    n_im = ab_im
    f_re = (n_re * lre + n_im * lim) / den
    f_im = (n_im * lre - n_re * lim) / den
    b_re = bre_ref[...]
    b_im = bim_ref[...]
    are_ref[...] = ab_re
    aim_ref[...] = ab_im
    g, hh, p = b_re.shape
    rows2d = lambda t: t.reshape(g * hh, p).astype(BF16)
    bb = (rows2d(f_re * b_re - f_im * b_im), rows2d(f_re * b_im + f_im * b_re))
    cc = (rows2d(cre_ref[...]), rows2d(-cim_ref[...]))

    nsb = wb_ref.shape[0]
    blk_rows = S5_BLOCK_GROUPS * hh
    ri = lax.broadcasted_iota(jnp.int32, (p, S5_BLOCK_S), 0)
    ci = lax.broadcasted_iota(jnp.int32, (p, S5_BLOCK_S), 1)
    spread = jnp.where(ci % p == ri, 1.0, 0.0).astype(BF16)
    ri = lax.broadcasted_iota(jnp.int32, (S5_BLOCK_S, p), 0)
    ci = lax.broadcasted_iota(jnp.int32, (S5_BLOCK_S, p), 1)
    spread_t = jnp.where(ri % p == ci, 1.0, 0.0).astype(BF16)
    ri = lax.broadcasted_iota(jnp.int32, (blk_rows, S5_BLOCK_S), 0)
    ci = lax.broadcasted_iota(jnp.int32, (blk_rows, S5_BLOCK_S), 1)
    diag_b = (ri // hh) == (ci // p)
    ri = lax.broadcasted_iota(jnp.int32, (S5_BLOCK_S, blk_rows), 0)
    ci = lax.broadcasted_iota(jnp.int32, (S5_BLOCK_S, blk_rows), 1)
    diag_c = (ri // p) == (ci // hh)
    for sb in range(nsb):
        for hf in range(2):
            r0 = (hf * (g // 2) + sb * S5_BLOCK_GROUPS) * hh
            for part in range(2):
                y = jnp.where(diag_b, _dot(bb[part][r0:r0 + blk_rows], spread), 0.0)
                wb_ref[sb, hf * blk_rows:(hf + 1) * blk_rows,
                       part * S5_BLOCK_S:(part + 1) * S5_BLOCK_S] = y.astype(BF16)
                z = jnp.where(diag_c, _dot_nt(spread_t, cc[part][r0:r0 + blk_rows]), 0.0)
                wc_ref[sb, part * S5_BLOCK_S:(part + 1) * S5_BLOCK_S,
                       hf * blk_rows:(hf + 1) * blk_rows] = z.astype(BF16)


def _s5prep(lam_re, lam_im, log_dt, bt_re, bt_im, c_re, c_im):
    g, p = lam_re.shape
    nsb = g // (2 * S5_BLOCK_GROUPS)
    a_shape = jax.ShapeDtypeStruct((g, 1, p), F32)
    return pl.pallas_call(
        _s5prep_kernel,
        out_shape=(a_shape, a_shape,
                   jax.ShapeDtypeStruct((nsb, 2 * S5_BLOCK_U, 2 * S5_BLOCK_S), BF16),
                   jax.ShapeDtypeStruct((nsb, 2 * S5_BLOCK_S, 2 * S5_BLOCK_U), BF16)),
        compiler_params=pltpu.CompilerParams(vmem_limit_bytes=VMEM_LIMIT),
        name="s5prep",
    )(lam_re.reshape(g, 1, p), lam_im.reshape(g, 1, p), log_dt.reshape(g, 1, 1), bt_re, bt_im, c_re, c_im)


def _s5_kernel(u0_ref, u1_ref, wb_ref, wc_ref, a_ref, d_ref, o_ref, uf_ref, bu_ref, s_ref, y_ref, carry_ref, *, tc):
    @pl.when(pl.program_id(1) == 0)
    def _():
        carry_ref[...] = jnp.zeros_like(carry_ref)

    nb = u0_ref.shape[0]
    tile = SUBLANES * S5_SUB
    for b in range(nb):
        uf_ref[pl.ds(b, tc, stride=SUBLANES), :] = u0_ref[b].astype(F32)
        uf_ref[pl.ds(nb + b, tc, stride=SUBLANES), :] = u1_ref[b].astype(F32)

    lo = (lax.broadcasted_iota(jnp.int32, (tile, S5_BLOCK_U), 0) & nb) == 0
    a_re = a_ref[0, :, :S5_BLOCK_S]
    a_im = a_ref[0, :, S5_BLOCK_S:]
    s_re = carry_ref[:, :S5_BLOCK_S]
    s_im = carry_ref[:, S5_BLOCK_S:]

    for s in range(tc // S5_SUB):
        base = s * tile
        uf = uf_ref[base:base + tile, :]
        lhs = jnp.concatenate([jnp.where(lo, uf, 0.0), jnp.where(lo, 0.0, uf)], axis=1).astype(BF16)
        bu_ref[base:base + tile, :] = _dot(lhs, wb_ref[0])
        for p in range(S5_SUB // 2):
            r0 = base + p * 2 * SUBLANES
            r1 = r0 + SUBLANES
            m_re = a_re * s_re - a_im * s_im + bu_ref[r0:r1, :S5_BLOCK_S]
            m_im = a_re * s_im + a_im * s_re + bu_ref[r0:r1, S5_BLOCK_S:]
            s_re = a_re * m_re - a_im * m_im + bu_ref[r1:r1 + SUBLANES, :S5_BLOCK_S]
            s_im = a_re * m_im + a_im * m_re + bu_ref[r1:r1 + SUBLANES, S5_BLOCK_S:]
            s_ref[r0:r0 + 2 * SUBLANES, :S5_BLOCK_S] = jnp.concatenate([m_re, s_re], axis=0).astype(BF16)
            s_ref[r0:r0 + 2 * SUBLANES, S5_BLOCK_S:] = jnp.concatenate([m_im, s_im], axis=0).astype(BF16)
        y8 = _dot(s_ref[base:base + tile, :], wc_ref[0])
        y = jnp.where(lo, y8[:, :S5_BLOCK_U], y8[:, S5_BLOCK_U:])
        y = (y.reshape(S5_SUB, SUBLANES, S5_BLOCK_U)
             + d_ref[...] * uf.reshape(S5_SUB, SUBLANES, S5_BLOCK_U)).reshape(tile, S5_BLOCK_U)
        cdf = 0.5 * (1.0 + jnp.tanh(math.sqrt(2.0 / math.pi) * (y + 0.044715 * (y * y * y))))
        y_ref[base:base + tile, :] = y * cdf

    carry_ref[:, :S5_BLOCK_S] = s_re
    carry_ref[:, S5_BLOCK_S:] = s_im
    for j in range(2 * nb):
        o_ref[j] = y_ref[pl.ds(j, tc, stride=SUBLANES), :].astype(o_ref.dtype)


def _s5(proj3, wb, wc, a8, d8):
    bsz, seq, _ = proj3.shape
    nsb = wb.shape[0]
    tc = min(512, seq)
    rows = SUBLANES * tc
    cb = COL_U // S5_BLOCK_U
    return pl.pallas_call(
        functools.partial(_s5_kernel, tc=tc),
        grid=(nsb, seq // tc),
        in_specs=[pl.BlockSpec((bsz, tc, S5_BLOCK_U), lambda sb, i: (0, i, cb + sb)),
                  pl.BlockSpec((bsz, tc, S5_BLOCK_U), lambda sb, i: (0, i, cb + nsb + sb)),
                  pl.BlockSpec((1, 2 * S5_BLOCK_U, 2 * S5_BLOCK_S), lambda sb, i: (sb, 0, 0)),
                  pl.BlockSpec((1, 2 * S5_BLOCK_S, 2 * S5_BLOCK_U), lambda sb, i: (sb, 0, 0)),
                  pl.BlockSpec((1, SUBLANES, 2 * S5_BLOCK_S), lambda sb, i: (sb, 0, 0)),
                  pl.BlockSpec((1, SUBLANES, S5_BLOCK_U), lambda sb, i: (sb, 0, 0))],
        out_specs=pl.BlockSpec((2 * bsz, tc, S5_BLOCK_U), lambda sb, i: (0, i, sb)),
        out_shape=jax.ShapeDtypeStruct((2 * bsz, seq, nsb * S5_BLOCK_U), BF16),
        scratch_shapes=[pltpu.VMEM((rows, S5_BLOCK_U), F32),
                        pltpu.VMEM((rows, 2 * S5_BLOCK_S), F32),
                        pltpu.VMEM((rows, 2 * S5_BLOCK_S), BF16),
                        pltpu.VMEM((rows, S5_BLOCK_U), F32),
                        pltpu.VMEM((SUBLANES, 2 * S5_BLOCK_S), F32)],
        compiler_params=_params("parallel", "arbitrary"),
        name="s5",
    )(proj3, proj3, wb, wc, a8, d8)


def _glu_kernel(y0_ref, y1_ref, z_ref, w_ref, b_ref, o_ref):
    yb = jnp.concatenate([y0_ref[0], y1_ref[0]], axis=1)
    acc = _dot(yb, w_ref[...]) + b_ref[...]
    y = yb.astype(F32)
    z = z_ref[...].astype(F32)
    o_ref[...] = (y * z / ((1.0 + jnp.exp(-acc)) * (1.0 + jnp.exp(-z)))).astype(o_ref.dtype)


def _glu(yv, proj, w, b, bsz, seq):
    d = w.shape[0]
    half = d // 2
    tm = min(1024, seq)
    nt = seq // tm
    return pl.pallas_call(
        _glu_kernel,
        grid=(bsz, nt),
        in_specs=[pl.BlockSpec((1, tm, half), lambda b_, i: (b_, i, 0)),
                  pl.BlockSpec((1, tm, half), lambda b_, i: (bsz + b_, i, 0)),
                  pl.BlockSpec((tm, d), lambda b_, i: (b_ * nt + i, COL_ZS // D_S5)),
                  pl.BlockSpec((d, d), lambda b_, i: (0, 0)),
                  pl.BlockSpec((1, d), lambda b_, i: (0, 0))],
        out_specs=pl.BlockSpec((tm, d), lambda b_, i: (b_ * nt + i, 0)),
        out_shape=jax.ShapeDtypeStruct((bsz * seq, d), BF16),
        compiler_params=_params("parallel", "parallel"),
        name="glu",
    )(yv, yv, proj, w, b)


OUT_EPILOGUE_CHUNKS = 2


def _out_kernel(og_ref, os_ref, w_ref, x_hbm, gate_ref, lg_ref, lb_ref, o_ref, x_buf, x_sem, *, nk, tm):
    b = pl.program_id(0)
    i = pl.program_id(1)
    k = pl.program_id(2)
    half = nk // 2

    def x_copy():
        r0 = pl.multiple_of(i * tm, tm)
        return pltpu.make_async_copy(x_hbm.at[b, pl.ds(r0, tm), :], x_buf, x_sem)

    @pl.when(k == min(1, nk - 1))
    def _():
        x_copy().start()

    @pl.when(k == 0)
    def _():
        o_ref[0] = _dot(og_ref[...], w_ref[...])

    @pl.when((k > 0) & (k < half))
    def _():
        o_ref[0] += _dot(og_ref[...], w_ref[...])

    @pl.when((k >= half) & (k < nk - 1))
    def _():
        o_ref[0] += _dot(os_ref[...], w_ref[...])

    @pl.when(k == nk - 1)
    def _():
        x_copy().wait()
        rows = tm // OUT_EPILOGUE_CHUNKS
        for c in range(OUT_EPILOGUE_CHUNKS):
            rs = slice(c * rows, (c + 1) * rows)
            mixed = o_ref[0, rs, :] + _dot(os_ref[rs, :], w_ref[...])
            r = DEEPNORM_ALPHA * x_buf[rs, :] + gate_ref[0] * mixed
            mu = jnp.mean(r, axis=-1, keepdims=True)
            rc = r - mu
            var = jnp.mean(rc * rc, axis=-1, keepdims=True)
            o_ref[0, rs, :] = rc * lax.rsqrt(var + NORM_EPS) * lg_ref[...] + lb_ref[...]


def _outproj(og, osb, w, x, mod3, lg, lb):
    bsz, seq, d = x.shape
    dh = og.shape[1]
    tm = min(512, seq)
    nt = seq // tm
    tk = 1024
    half = dh // tk
    nk = 2 * half
    return pl.pallas_call(
        functools.partial(_out_kernel, nk=nk, tm=tm),
        grid=(bsz, nt, nk),
        in_specs=[pl.BlockSpec((tm, tk), lambda b, i, k: (b * nt + i, jnp.minimum(k, half - 1))),
                  pl.BlockSpec((tm, tk), lambda b, i, k: (b * nt + i, jnp.maximum(k - half, 0))),
                  pl.BlockSpec((tk, d), lambda b, i, k: (k, 0)),
                  pl.BlockSpec(memory_space=pl.ANY),
                  pl.BlockSpec((1, 1, d), lambda b, i, k: (b, 0, 2)),
                  pl.BlockSpec((1, d), lambda b, i, k: (0, 0)),
                  pl.BlockSpec((1, d), lambda b, i, k: (0, 0))],
        out_specs=pl.BlockSpec((1, tm, d), lambda b, i, k: (b, i, 0)),
        out_shape=jax.ShapeDtypeStruct(x.shape, x.dtype),
        scratch_shapes=[pltpu.VMEM((tm, d), F32), pltpu.SemaphoreType.DMA(())],
        compiler_params=_params("parallel", "parallel", "arbitrary"),
        name="outproj",
    )(og, osb, w, x, mod3, lg, lb)


def _s5_layouts(ab_re, ab_im, d_skip, bsz):
    g = ab_re.shape[0]
    nsb = g // (2 * S5_BLOCK_GROUPS)

    def rows8(v, width):
        t = v.reshape(2, nsb, 1, width).transpose(1, 0, 2, 3)
        return jnp.broadcast_to(t, (nsb, 2, bsz, width)).reshape(nsb, 2 * bsz, width)

    a8 = jnp.concatenate([rows8(ab_re.reshape(-1), S5_BLOCK_S), rows8(ab_im.reshape(-1), S5_BLOCK_S)], axis=2)
    d8 = rows8(d_skip, S5_BLOCK_U)
    return a8, d8


def kernel(x, c, w_ada, b_ada, w_in, w_gla_gate, b_gla_gate, gla_norm_g, s5_lambda_re, s5_lambda_im, s5_log_dt, s5_b_re, s5_b_im, s5_c_re, s5_c_im, s5_d, w_glu, b_glu, w_out, ln_g, ln_b):
    bsz, seq, d = x.shape
    assert bsz * 2 == SUBLANES and w_ada.shape[0] == DEPTH and s5_d.shape[1] == D_S5
    layer = 0

    mod = _ada(c.T, w_ada[layer], b_ada[layer][None, :])
    mod3 = mod.reshape(bsz, 1, 3 * d)

    w_main, w_glr = _wprep(jnp.swapaxes(w_in, 1, 2)[layer])
    proj, g_lr = _inproj(x, mod3, w_main, w_glr)

    wg = jnp.pad(w_gla_gate[layer], ((0, LANES - GLA_GATE_RANK), (0, 0))).astype(BF16)
    o_gla = _gla(proj, g_lr, wg, b_gla_gate[layer][None, :], gla_norm_g[layer][None, :], bsz, seq)

    bt_re = s5_b_re[layer].transpose(0, 2, 1)
    bt_im = s5_b_im[layer].transpose(0, 2, 1)
    ab_re, ab_im, wb, wc = _s5prep(s5_lambda_re[layer], s5_lambda_im[layer], s5_log_dt[layer],
                                   bt_re, bt_im, s5_c_re[layer], s5_c_im[layer])
    a8, d8 = _s5_layouts(ab_re, ab_im, s5_d[layer], bsz)
    yv = _s5(proj.reshape(bsz, seq, N_PROJ), wb, wc, a8, d8)
    o_s5 = _glu(yv, proj, w_glu[layer].astype(BF16), b_glu[layer][None, :], bsz, seq)

    return _outproj(o_gla, o_s5, w_out[layer].astype(BF16), x, mod3,
                    ln_g[layer][None, :], ln_b[layer][None, :])
```

```python
import functools
import itertools
import math

import jax
import jax.numpy as jnp
from jax import lax
from jax.experimental import pallas as pl
from jax.experimental.pallas import tpu as pltpu

F32 = jnp.float32
BF16 = jnp.bfloat16

GLA_HEADS = 4
GLA_DK = 256
GLA_DV = 512
GLA_GATE_RANK = 16
GLA_GATE_TAU = 16.0
GLA_CHUNK = 64
S5_GROUP = 16
S5_STATE = 64
NORM_EPS = 1e-5
DEPTH = 1
DEEPNORM_ALPHA = (2.0 * DEPTH) ** 0.25

LANES = 128
SUBLANES = 8
VMEM_LIMIT = 60 * 1024 * 1024

DK_TOT = GLA_HEADS * GLA_DK
D_GLA = GLA_HEADS * GLA_DV
D_S5 = 2048
COL_Q = 0
COL_K = COL_Q + DK_TOT
COL_V = COL_K + DK_TOT
COL_ZG = COL_V + D_GLA
COL_U = COL_ZG + D_GLA
COL_ZS = COL_U + D_S5
N_PROJ = COL_ZS + D_S5
MXU_WIDTH = 256
INPROJ_TN = 5 * MXU_WIDTH

S5_BLOCK_GROUPS = 8
S5_BLOCK_U = S5_BLOCK_GROUPS * S5_GROUP
S5_BLOCK_S = S5_BLOCK_GROUPS * S5_STATE
S5_SUB = 16


def _sigmoid(v):
    return 0.5 * (1.0 + jnp.tanh(0.5 * v))


def _silu(v):
    return v * _sigmoid(v)


def _dot(a, b):
    return jnp.dot(a, b, preferred_element_type=F32)


def _dot_nt(a, b):
    return lax.dot_general(a, b, (((1,), (1,)), ((), ())), preferred_element_type=F32)


def _dot_tn(a, b):
    return lax.dot_general(a, b, (((0,), (0,)), ((), ())), preferred_element_type=F32)


def _params(*sem):
    return pltpu.CompilerParams(dimension_semantics=sem, vmem_limit_bytes=VMEM_LIMIT)


def _ada_kernel(ct_ref, w_ref, b_ref, o_ref, sc_ref):
    nb = sc_ref.shape[0]
    d, tn = w_ref.shape

    @pl.when(pl.program_id(0) == 0)
    def _():
        sc = _silu(ct_ref[...])
        for b in range(nb):
            sc_ref[b] = jnp.broadcast_to(sc[:, b:b + 1], (d, LANES))

    def body(g, accs):
        r0 = pl.multiple_of(g * SUBLANES, SUBLANES)
        w = w_ref[pl.ds(r0, SUBLANES), :]
        return tuple(acc + w * jnp.concatenate([sc_ref[b, pl.ds(r0, SUBLANES), :]] * (tn // LANES), axis=1)
                     for b, acc in enumerate(accs))

    accs = lax.fori_loop(0, d // SUBLANES, body,
                         tuple(jnp.zeros((SUBLANES, tn), F32) for _ in range(nb)), unroll=8)
    for b in range(nb):
        o_ref[b:b + 1, :] = jnp.sum(accs[b], axis=0, keepdims=True) + b_ref[...]


def _ada(ct, w, b):
    d, n = w.shape
    nb = ct.shape[1]
    tn = 1024
    return pl.pallas_call(
        _ada_kernel,
        grid=(n // tn,),
        in_specs=[pl.BlockSpec((d, nb), lambda j: (0, 0)),
                  pl.BlockSpec((d, tn), lambda j: (0, j)),
                  pl.BlockSpec((1, tn), lambda j: (0, j))],
        out_specs=pl.BlockSpec((nb, tn), lambda j: (0, j)),
        out_shape=jax.ShapeDtypeStruct((nb, n), F32),
        scratch_shapes=[pltpu.VMEM((nb, d, LANES), F32)],
        compiler_params=_params("arbitrary"),
        name="ada",
    )(ct, w, b)


WPREP_ROWS = 512


def _wprep_kernel(cur_ref, nxt_ref, o_ref, g_ref, *, n_plain):
    i = pl.program_id(0)
    keep = WPREP_ROWS - GLA_GATE_RANK

    @pl.when(i < n_plain)
    def _():
        o_ref[...] = cur_ref[...].astype(BF16)

    @pl.when(i >= n_plain)
    def _():
        o_ref[:keep] = cur_ref[GLA_GATE_RANK:].astype(BF16)
        o_ref[keep:] = nxt_ref[...].astype(BF16)

    @pl.when(i == n_plain - 1)
    def _():
        g_ref[:GLA_GATE_RANK] = nxt_ref[...].astype(BF16)
        g_ref[GLA_GATE_RANK:] = jnp.zeros((LANES - GLA_GATE_RANK, g_ref.shape[1]), BF16)


def _wprep(wt):
    n, d = wt.shape
    assert n == N_PROJ + GLA_GATE_RANK and COL_ZG % WPREP_ROWS == 0 and N_PROJ % WPREP_ROWS == 0
    per = WPREP_ROWS // GLA_GATE_RANK
    return pl.pallas_call(
        functools.partial(_wprep_kernel, n_plain=COL_ZG // WPREP_ROWS),
        grid=(N_PROJ // WPREP_ROWS,),
        in_specs=[pl.BlockSpec((WPREP_ROWS, d), lambda i: (i, 0)),
                  pl.BlockSpec((GLA_GATE_RANK, d), lambda i: ((i + 1) * per, 0))],
        out_specs=(pl.BlockSpec((WPREP_ROWS, d), lambda i: (i, 0)),
                   pl.BlockSpec((LANES, d), lambda i: (0, 0))),
        out_shape=(jax.ShapeDtypeStruct((N_PROJ, d), BF16),
                   jax.ShapeDtypeStruct((LANES, d), BF16)),
        compiler_params=_params("arbitrary"),
        name="wprep",
    )(wt, wt)


def _inproj_kernel(x_hbm, shift_ref, scale_ref, w_ref, wg_ref, o_ref, g_ref, h_ref, x_buf, x_sem, *, tm, nt, n_tiles):
    b = pl.program_id(0)
    i = pl.program_id(1)

    def x_copy(tile):
        r0 = pl.multiple_of((tile % nt) * tm, tm)
        return pltpu.make_async_copy(x_hbm.at[tile // nt, pl.ds(r0, tm), :], x_buf, x_sem)

    @pl.when(pl.program_id(2) == 0)
    def _():
        tile = b * nt + i

        @pl.when(tile == 0)
        def _():
            x_copy(tile).start()

        x_copy(tile).wait()
        h_ref[...] = (x_buf[...] * (1.0 + scale_ref[0]) + shift_ref[0]).astype(BF16)

        @pl.when(tile + 1 < n_tiles)
        def _():
            x_copy(tile + 1).start()

        g_ref[...] = _dot_nt(h_ref[...], wg_ref[...]).astype(g_ref.dtype)

    o_ref[...] = _dot_nt(h_ref[...], w_ref[...]).astype(o_ref.dtype)


def _inproj(x, mod3, w, wg):
    bsz, seq, d = x.shape
    n = w.shape[0]
    tn = INPROJ_TN
    tm = min(1024, seq)
    nt = seq // tm
    return pl.pallas_call(
        functools.partial(_inproj_kernel, tm=tm, nt=nt, n_tiles=bsz * nt),
        grid=(bsz, nt, n // tn),
        in_specs=[pl.BlockSpec(memory_space=pl.ANY),
                  pl.BlockSpec((1, 1, d), lambda b, i, j: (b, 0, 0)),
                  pl.BlockSpec((1, 1, d), lambda b, i, j: (b, 0, 1)),
                  pl.BlockSpec((tn, d), lambda b, i, j: (j, 0)),
                  pl.BlockSpec((LANES, d), lambda b, i, j: (0, 0))],
        out_specs=(pl.BlockSpec((tm, tn), lambda b, i, j: (b * nt + i, j)),
                   pl.BlockSpec((tm, LANES), lambda b, i, j: (b * nt + i, 0))),
        out_shape=(jax.ShapeDtypeStruct((bsz * seq, n), BF16),
                   jax.ShapeDtypeStruct((bsz * seq, LANES), BF16)),
        scratch_shapes=[pltpu.VMEM((tm, d), BF16), pltpu.VMEM((tm, d), F32), pltpu.SemaphoreType.DMA(())],
        compiler_params=_params("arbitrary", "arbitrary", "arbitrary"),
        name="inproj",
    )(x, mod3, mod3, w, wg)


GLA_BLOCK = 4 * GLA_CHUNK
GLA_NCHUNK = GLA_BLOCK // GLA_CHUNK
GLA_STEP_BLOCKS = 4


def _gla_constants():
    import numpy as np
    tc = GLA_BLOCK
    row = np.arange(tc)[:, None]
    col = np.arange(tc)[None, :]
    rc = row // GLA_CHUNK
    cc = col // GLA_CHUNK
    half = GLA_NCHUNK // 2
    same = rc == cc
    totals = np.arange(2 * SUBLANES)[:, None] == cc
    sums = np.concatenate([same & (col <= row), totals], axis=0)
    sel = np.where(same & (col <= row), 1.0,
                   np.where((rc == cc + 1) & (rc != half), 2.0,
                            np.where((rc >= half) & (cc < half), 3.0, 0.0)))
    return jnp.asarray(sums.astype(np.float32), dtype=BF16), jnp.asarray(sel.astype(np.float32))


def _gla_kernel(q_ref, k_ref, g_ref, v_ref, z_ref, wg_ref, bg_ref, ng_ref, sums_ref, sel_ref, o_ref, st_ref):
    @pl.when(pl.program_id(1) == 0)
    def _():
        st_ref[...] = jnp.zeros_like(st_ref)

    tc = GLA_BLOCK
    sel = sel_ref[...]
    logit = _dot(g_ref[...], wg_ref[...]) + bg_ref[...]
    la_all = (jnp.minimum(logit, 0.0) - jnp.log(1.0 + jnp.exp(-jnp.abs(logit)))) * (1.0 / GLA_GATE_TAU)
    for blk, hd in itertools.product(range(GLA_STEP_BLOCKS), range(GLA_HEADS)):
        rs = slice(blk * tc, (blk + 1) * tc)
        ck = slice(hd * GLA_DK, (hd + 1) * GLA_DK)
        cv = slice(hd * GLA_DV, (hd + 1) * GLA_DV)
        la = la_all[rs, ck]
        la_hi = la.astype(BF16)
        la_lo = (la - la_hi.astype(F32)).astype(BF16)
        sums = _dot(sums_ref[...], la_hi) + _dot(sums_ref[...], la_lo)
        b = sums[:tc]
        tot = [sums[tc + c:tc + c + 1] for c in range(GLA_NCHUNK)]
        half = GLA_NCHUNK // 2

        def span(lo, hi):
            return sum(tot[lo:hi]) if hi > lo else jnp.zeros_like(tot[0])

        k = k_ref[rs, ck].astype(F32)
        q_dec = q_ref[rs, ck].astype(F32) * (GLA_DK ** -0.5) * jnp.exp(b)
        k_inv = k * jnp.exp(-b)
        k_end, q_in, k_st, q_mid, k_mid = [], [], [], [], []
        for c in range(GLA_NCHUNK):
            rows = slice(c * GLA_CHUNK, (c + 1) * GLA_CHUNK)
            ke = k[rows] * jnp.exp(tot[c] - b[rows])
            e_mid = jnp.exp(span(half, c) if c >= half else span(c + 1, half))
            k_end.append(ke)
            q_in.append(q_dec[rows] * jnp.exp(span(0, c)))
            k_st.append(ke * jnp.exp(span(c + 1, GLA_NCHUNK)))
            q_mid.append(q_dec[rows] * e_mid)
            k_mid.append(ke * e_mid)
        cat = lambda parts: jnp.concatenate(parts, axis=0).astype(BF16)
        a_same = _dot_nt(q_dec.astype(BF16), k_inv.astype(BF16))
        a_next = _dot_nt(q_dec.astype(BF16), cat(k_end))
        a_mid = _dot_nt(cat(q_mid), cat(k_mid))
        att = jnp.where(sel == 1.0, a_same,
                        jnp.where(sel == 2.0, a_next, jnp.where(sel == 3.0, a_mid, 0.0))).astype(BF16)
        v = v_ref[rs, cv]
        st = st_ref[hd]
        o = _dot(att, v) + _dot_nt(cat(q_in), st.astype(BF16))
        st_ref[hd] = st * jnp.exp(span(0, GLA_NCHUNK)) + _dot_tn(v, cat(k_st))
        o = o * lax.rsqrt(jnp.mean(o * o, axis=-1, keepdims=True) + NORM_EPS) * ng_ref[...]
        z = z_ref[rs, cv].astype(F32)
        o_ref[rs, cv] = (o * _silu(z)).astype(o_ref.dtype)


def _gla(proj, g_lr, wg, bg, ng, bsz, seq):
    tc = GLA_BLOCK * GLA_STEP_BLOCKS
    nt = seq // tc
    row = lambda b, i: b * nt + i
    sums, sel = _gla_constants()
    const = lambda b, i: (0, 0)
    return pl.pallas_call(
        _gla_kernel,
        grid=(bsz, nt),
        in_specs=[pl.BlockSpec((tc, DK_TOT), lambda b, i: (row(b, i), COL_Q // DK_TOT)),
                  pl.BlockSpec((tc, DK_TOT), lambda b, i: (row(b, i), COL_K // DK_TOT)),
                  pl.BlockSpec((tc, LANES), lambda b, i: (row(b, i), 0)),
                  pl.BlockSpec((tc, D_GLA), lambda b, i: (row(b, i), COL_V // D_GLA)),
                  pl.BlockSpec((tc, D_GLA), lambda b, i: (row(b, i), COL_ZG // D_GLA)),
                  pl.BlockSpec((LANES, DK_TOT), const),
                  pl.BlockSpec((1, DK_TOT), const),
                  pl.BlockSpec((1, GLA_DV), const),
                  pl.BlockSpec((GLA_BLOCK + 2 * SUBLANES, GLA_BLOCK), const),
                  pl.BlockSpec((GLA_BLOCK, GLA_BLOCK), const)],
        out_specs=pl.BlockSpec((tc, D_GLA), lambda b, i: (row(b, i), 0)),
        out_shape=jax.ShapeDtypeStruct((bsz * seq, D_GLA), BF16),
        scratch_shapes=[pltpu.VMEM((GLA_HEADS, GLA_DV, GLA_DK), F32)],
        compiler_params=_params("parallel", "arbitrary"),
        name="gla",
    )(proj, proj, g_lr, proj, proj, wg, bg, ng, sums, sel)


def _s5prep_kernel(lre_ref, lim_ref, ldt_ref, bre_ref, bim_ref, cre_ref, cim_ref,
                   are_ref, aim_ref, wb_ref, wc_ref):
    lre = lre_ref[...]
    lim = lim_ref[...]
    dt = jnp.exp(ldt_ref[...])
    z_re = lre * dt
    z_im = lim * dt
    mag = jnp.exp(z_re)
    ab_re = mag * jnp.cos(z_im)
    ab_im = mag * jnp.sin(z_im)
    den = lre * lre + lim * lim
    n_re = ab_re - 1.0
    n_im = ab_im
    f_re = (n_re * lre + n_im * lim) / den
    f_im = (n_im * lre - n_re * lim) / den
    b_re = bre_ref[...]
    b_im = bim_ref[...]
    are_ref[...] = ab_re
    aim_ref[...] = ab_im
    g, hh, p = b_re.shape
    rows2d = lambda t: t.reshape(g * hh, p).astype(BF16)
    bb = (rows2d(f_re * b_re - f_im * b_im), rows2d(f_re * b_im + f_im * b_re))
    cc = (rows2d(cre_ref[...]), rows2d(-cim_ref[...]))

    nsb = wb_ref.shape[0]
    blk_rows = S5_BLOCK_GROUPS * hh
    ri = lax.broadcasted_iota(jnp.int32, (p, S5_BLOCK_S), 0)
    ci = lax.broadcasted_iota(jnp.int32, (p, S5_BLOCK_S), 1)
    spread = jnp.where(ci % p == ri, 1.0, 0.0).astype(BF16)
    ri = lax.broadcasted_iota(jnp.int32, (S5_BLOCK_S, p), 0)
    ci = lax.broadcasted_iota(jnp.int32, (S5_BLOCK_S, p), 1)
    spread_t = jnp.where(ri % p == ci, 1.0, 0.0).astype(BF16)
    ri = lax.broadcasted_iota(jnp.int32, (blk_rows, S5_BLOCK_S), 0)
    ci = lax.broadcasted_iota(jnp.int32, (blk_rows, S5_BLOCK_S), 1)
    diag_b = (ri // hh) == (ci // p)
    ri = lax.broadcasted_iota(jnp.int32, (S5_BLOCK_S, blk_rows), 0)
    ci = lax.broadcasted_iota(jnp.int32, (S5_BLOCK_S, blk_rows), 1)
    diag_c = (ri // p) == (ci // hh)
    for sb in range(nsb):
        for hf in range(2):
            r0 = (hf * (g // 2) + sb * S5_BLOCK_GROUPS) * hh
            for part in range(2):
                y = jnp.where(diag_b, _dot(bb[part][r0:r0 + blk_rows], spread), 0.0)
                wb_ref[sb, hf * blk_rows:(hf + 1) * blk_rows,
                       part * S5_BLOCK_S:(part + 1) * S5_BLOCK_S] = y.astype(BF16)
                z = jnp.where(diag_c, _dot_nt(spread_t, cc[part][r0:r0 + blk_rows]), 0.0)
                wc_ref[sb, part * S5_BLOCK_S:(part + 1) * S5_BLOCK_S,
                       hf * blk_rows:(hf + 1) * blk_rows] = z.astype(BF16)


def _s5prep(lam_re, lam_im, log_dt, bt_re, bt_im, c_re, c_im):
    g, p = lam_re.shape
    nsb = g // (2 * S5_BLOCK_GROUPS)
    a_shape = jax.ShapeDtypeStruct((g, 1, p), F32)
    return pl.pallas_call(
        _s5prep_kernel,
        out_shape=(a_shape, a_shape,
                   jax.ShapeDtypeStruct((nsb, 2 * S5_BLOCK_U, 2 * S5_BLOCK_S), BF16),
                   jax.ShapeDtypeStruct((nsb, 2 * S5_BLOCK_S, 2 * S5_BLOCK_U), BF16)),
        compiler_params=pltpu.CompilerParams(vmem_limit_bytes=VMEM_LIMIT),
        name="s5prep",
    )(lam_re.reshape(g, 1, p), lam_im.reshape(g, 1, p), log_dt.reshape(g, 1, 1), bt_re, bt_im, c_re, c_im)


def _s5_kernel(u0_ref, u1_ref, wb_ref, wc_ref, a_ref, d_ref, o_ref, uf_ref, bu_ref, s_ref, y_ref, carry_ref, *, tc):
    @pl.when(pl.program_id(1) == 0)
    def _():
        carry_ref[...] = jnp.zeros_like(carry_ref)

    nb = u0_ref.shape[0]
    tile = SUBLANES * S5_SUB
    for b in range(nb):
        uf_ref[pl.ds(b, tc, stride=SUBLANES), :] = u0_ref[b].astype(F32)
        uf_ref[pl.ds(nb + b, tc, stride=SUBLANES), :] = u1_ref[b].astype(F32)

    lo = (lax.broadcasted_iota(jnp.int32, (tile, S5_BLOCK_U), 0) & nb) == 0
    a_re = a_ref[0, :, :S5_BLOCK_S]
    a_im = a_ref[0, :, S5_BLOCK_S:]
    s_re = carry_ref[:, :S5_BLOCK_S]
    s_im = carry_ref[:, S5_BLOCK_S:]

    for s in range(tc // S5_SUB):
        base = s * tile
        uf = uf_ref[base:base + tile, :]
        lhs = jnp.concatenate([jnp.where(lo, uf, 0.0), jnp.where(lo, 0.0, uf)], axis=1).astype(BF16)
        bu_ref[base:base + tile, :] = _dot(lhs, wb_ref[0])
        for p in range(S5_SUB // 2):
            r0 = base + p * 2 * SUBLANES
            r1 = r0 + SUBLANES
            m_re = a_re * s_re - a_im * s_im + bu_ref[r0:r1, :S5_BLOCK_S]
            m_im = a_re * s_im + a_im * s_re + bu_ref[r0:r1, S5_BLOCK_S:]
            s_re = a_re * m_re - a_im * m_im + bu_ref[r1:r1 + SUBLANES, :S5_BLOCK_S]
            s_im = a_re * m_im + a_im * m_re + bu_ref[r1:r1 + SUBLANES, S5_BLOCK_S:]
            s_ref[r0:r0 + 2 * SUBLANES, :S5_BLOCK_S] = jnp.concatenate([m_re, s_re], axis=0).astype(BF16)
            s_ref[r0:r0 + 2 * SUBLANES, S5_BLOCK_S:] = jnp.concatenate([m_im, s_im], axis=0).astype(BF16)
        y8 = _dot(s_ref[base:base + tile, :], wc_ref[0])
        y = jnp.where(lo, y8[:, :S5_BLOCK_U], y8[:, S5_BLOCK_U:])
        y = (y.reshape(S5_SUB, SUBLANES, S5_BLOCK_U)
             + d_ref[...] * uf.reshape(S5_SUB, SUBLANES, S5_BLOCK_U)).reshape(tile, S5_BLOCK_U)
        cdf = 0.5 * (1.0 + jnp.tanh(math.sqrt(2.0 / math.pi) * (y + 0.044715 * (y * y * y))))
        y_ref[base:base + tile, :] = y * cdf

    carry_ref[:, :S5_BLOCK_S] = s_re
    carry_ref[:, S5_BLOCK_S:] = s_im
    for j in range(2 * nb):
        o_ref[j] = y_ref[pl.ds(j, tc, stride=SUBLANES), :].astype(o_ref.dtype)


def _s5(proj3, wb, wc, a8, d8):
    bsz, seq, _ = proj3.shape
    nsb = wb.shape[0]
    tc = min(512, seq)
    rows = SUBLANES * tc
    cb = COL_U // S5_BLOCK_U
    return pl.pallas_call(
        functools.partial(_s5_kernel, tc=tc),
        grid=(nsb, seq // tc),
        in_specs=[pl.BlockSpec((bsz, tc, S5_BLOCK_U), lambda sb, i: (0, i, cb + sb)),
                  pl.BlockSpec((bsz, tc, S5_BLOCK_U), lambda sb, i: (0, i, cb + nsb + sb)),
                  pl.BlockSpec((1, 2 * S5_BLOCK_U, 2 * S5_BLOCK_S), lambda sb, i: (sb, 0, 0)),
                  pl.BlockSpec((1, 2 * S5_BLOCK_S, 2 * S5_BLOCK_U), lambda sb, i: (sb, 0, 0)),
                  pl.BlockSpec((1, SUBLANES, 2 * S5_BLOCK_S), lambda sb, i: (sb, 0, 0)),
                  pl.BlockSpec((1, SUBLANES, S5_BLOCK_U), lambda sb, i: (sb, 0, 0))],
        out_specs=pl.BlockSpec((2 * bsz, tc, S5_BLOCK_U), lambda sb, i: (0, i, sb)),
        out_shape=jax.ShapeDtypeStruct((2 * bsz, seq, nsb * S5_BLOCK_U), BF16),
        scratch_shapes=[pltpu.VMEM((rows, S5_BLOCK_U), F32),
                        pltpu.VMEM((rows, 2 * S5_BLOCK_S), F32),
                        pltpu.VMEM((rows, 2 * S5_BLOCK_S), BF16),
                        pltpu.VMEM((rows, S5_BLOCK_U), F32),
                        pltpu.VMEM((SUBLANES, 2 * S5_BLOCK_S), F32)],
        compiler_params=_params("parallel", "arbitrary"),
        name="s5",
    )(proj3, proj3, wb, wc, a8, d8)


def _glu_kernel(y0_ref, y1_ref, z_ref, w_ref, b_ref, o_ref):
    yb = jnp.concatenate([y0_ref[0], y1_ref[0]], axis=1)
    acc = _dot(yb, w_ref[...]) + b_ref[...]
    y = yb.astype(F32)
    z = z_ref[...].astype(F32)
    o_ref[...] = (y * z * (0.25 * (1.0 + jnp.tanh(0.5 * acc)) * (1.0 + jnp.tanh(0.5 * z)))).astype(o_ref.dtype)


def _glu(yv, proj, w, b, bsz, seq):
    d = w.shape[0]
    half = d // 2
    tm = min(1024, seq)
    nt = seq // tm
    return pl.pallas_call(
        _glu_kernel,
        grid=(bsz, nt),
        in_specs=[pl.BlockSpec((1, tm, half), lambda b_, i: (b_, i, 0)),
                  pl.BlockSpec((1, tm, half), lambda b_, i: (bsz + b_, i, 0)),
                  pl.BlockSpec((tm, d), lambda b_, i: (b_ * nt + i, COL_ZS // D_S5)),
                  pl.BlockSpec((d, d), lambda b_, i: (0, 0)),
                  pl.BlockSpec((1, d), lambda b_, i: (0, 0))],
        out_specs=pl.BlockSpec((tm, d), lambda b_, i: (b_ * nt + i, 0)),
        out_shape=jax.ShapeDtypeStruct((bsz * seq, d), BF16),
        compiler_params=_params("parallel", "parallel"),
        name="glu",
    )(yv, yv, proj, w, b)


OUT_EPILOGUE_CHUNKS = 2


def _out_kernel(og_ref, os_ref, w_ref, x_hbm, gate_ref, lg_ref, lb_ref, o_ref, x_buf, x_sem, *, nk, tm):
    b = pl.program_id(0)
    i = pl.program_id(1)
    k = pl.program_id(2)
    half = nk // 2

    def x_copy():
        r0 = pl.multiple_of(i * tm, tm)
        return pltpu.make_async_copy(x_hbm.at[b, pl.ds(r0, tm), :], x_buf, x_sem)

    @pl.when(k == min(1, nk - 1))
    def _():
        x_copy().start()

    @pl.when(k == 0)
    def _():
        o_ref[0] = _dot(og_ref[...], w_ref[...])

    @pl.when((k > 0) & (k < half))
    def _():
        o_ref[0] += _dot(og_ref[...], w_ref[...])

    @pl.when((k >= half) & (k < nk - 1))
    def _():
        o_ref[0] += _dot(os_ref[...], w_ref[...])

    @pl.when(k == nk - 1)
    def _():
        x_copy().wait()
        rows = tm // OUT_EPILOGUE_CHUNKS
        gate = gate_ref[0] * (1.0 / DEEPNORM_ALPHA)
        for c in range(OUT_EPILOGUE_CHUNKS):
            rs = slice(c * rows, (c + 1) * rows)
            mixed = o_ref[0, rs, :] + _dot(os_ref[rs, :], w_ref[...])
            r = x_buf[rs, :] + gate * mixed
            mu = jnp.mean(r, axis=-1, keepdims=True)
            rc = r - mu
            var = jnp.mean(rc * rc, axis=-1, keepdims=True)
            o_ref[0, rs, :] = (rc * lax.rsqrt(var + NORM_EPS / DEEPNORM_ALPHA ** 2) * lg_ref[...]
                               + lb_ref[...])


def _outproj(og, osb, w, x, mod3, lg, lb):
    bsz, seq, d = x.shape
    dh = og.shape[1]
    tm = min(512, seq)
    nt = seq // tm
    tk = 1024
    half = dh // tk
    nk = 2 * half
    return pl.pallas_call(
        functools.partial(_out_kernel, nk=nk, tm=tm),
        grid=(bsz, nt, nk),
        in_specs=[pl.BlockSpec((tm, tk), lambda b, i, k: (b * nt + i, jnp.minimum(k, half - 1))),
                  pl.BlockSpec((tm, tk), lambda b, i, k: (b * nt + i, jnp.maximum(k - half, 0))),
                  pl.BlockSpec((tk, d), lambda b, i, k: (k, 0)),
                  pl.BlockSpec(memory_space=pl.ANY),
                  pl.BlockSpec((1, 1, d), lambda b, i, k: (b, 0, 2)),
                  pl.BlockSpec((1, d), lambda b, i, k: (0, 0)),
                  pl.BlockSpec((1, d), lambda b, i, k: (0, 0))],
        out_specs=pl.BlockSpec((1, tm, d), lambda b, i, k: (b, i, 0)),
        out_shape=jax.ShapeDtypeStruct(x.shape, x.dtype),
        scratch_shapes=[pltpu.VMEM((tm, d), F32), pltpu.SemaphoreType.DMA(())],
        compiler_params=_params("parallel", "parallel", "arbitrary"),
        name="outproj",
    )(og, osb, w, x, mod3, lg, lb)


def _s5_layouts(ab_re, ab_im, d_skip, bsz):
    g = ab_re.shape[0]
    nsb = g // (2 * S5_BLOCK_GROUPS)

    def rows8(v, width):
        t = v.reshape(2, nsb, 1, width).transpose(1, 0, 2, 3)
        return jnp.broadcast_to(t, (nsb, 2, bsz, width)).reshape(nsb, 2 * bsz, width)

    a8 = jnp.concatenate([rows8(ab_re.reshape(-1), S5_BLOCK_S), rows8(ab_im.reshape(-1), S5_BLOCK_S)], axis=2)
    d8 = rows8(d_skip, S5_BLOCK_U)
    return a8, d8


def kernel(x, c, w_ada, b_ada, w_in, w_gla_gate, b_gla_gate, gla_norm_g, s5_lambda_re, s5_lambda_im, s5_log_dt, s5_b_re, s5_b_im, s5_c_re, s5_c_im, s5_d, w_glu, b_glu, w_out, ln_g, ln_b):
    bsz, seq, d = x.shape
    assert bsz * 2 == SUBLANES and w_ada.shape[0] == DEPTH and s5_d.shape[1] == D_S5
    layer = 0

    mod = _ada(c.T, w_ada[layer], b_ada[layer][None, :])
    mod3 = mod.reshape(bsz, 1, 3 * d)

    w_main, w_glr = _wprep(jnp.swapaxes(w_in, 1, 2)[layer])
    proj, g_lr = _inproj(x, mod3, w_main, w_glr)

    wg = jnp.pad(w_gla_gate[layer], ((0, LANES - GLA_GATE_RANK), (0, 0))).astype(BF16)
    o_gla = _gla(proj, g_lr, wg, b_gla_gate[layer][None, :], gla_norm_g[layer][None, :], bsz, seq)

    bt_re = s5_b_re[layer].transpose(0, 2, 1)
    bt_im = s5_b_im[layer].transpose(0, 2, 1)
    ab_re, ab_im, wb, wc = _s5prep(s5_lambda_re[layer], s5_lambda_im[layer], s5_log_dt[layer],
                                   bt_re, bt_im, s5_c_re[layer], s5_c_im[layer])
    a8, d8 = _s5_layouts(ab_re, ab_im, s5_d[layer], bsz)
    yv = _s5(proj.reshape(bsz, seq, N_PROJ), wb, wc, a8, d8)
    o_s5 = _glu(yv, proj, w_glu[layer].astype(BF16), b_glu[layer][None, :], bsz, seq)

    return _outproj(o_gla, o_s5, w_out[layer].astype(BF16), x, mod3,
                    ln_g[layer][None, :], ln_b[layer][None, :])
```

```python
import functools
import itertools
import math

import jax
import jax.numpy as jnp
from jax import lax
from jax.experimental import pallas as pl
from jax.experimental.pallas import tpu as pltpu

F32 = jnp.float32
BF16 = jnp.bfloat16

GLA_HEADS = 4
GLA_DK = 256
GLA_DV = 512
GLA_GATE_RANK = 16
GLA_GATE_TAU = 16.0
GLA_CHUNK = 64
S5_GROUP = 16
S5_STATE = 64
NORM_EPS = 1e-5
DEPTH = 1
DEEPNORM_ALPHA = (2.0 * DEPTH) ** 0.25

LANES = 128
SUBLANES = 8
VMEM_LIMIT = 60 * 1024 * 1024

DK_TOT = GLA_HEADS * GLA_DK
D_GLA = GLA_HEADS * GLA_DV
D_S5 = 2048
COL_Q = 0
COL_K = COL_Q + DK_TOT
COL_V = COL_K + DK_TOT
COL_ZG = COL_V + D_GLA
COL_U = COL_ZG + D_GLA
COL_ZS = COL_U + D_S5
N_PROJ = COL_ZS + D_S5
MXU_WIDTH = 256
INPROJ_TN = 5 * MXU_WIDTH

S5_BLOCK_GROUPS = 8
S5_BLOCK_U = S5_BLOCK_GROUPS * S5_GROUP
S5_BLOCK_S = S5_BLOCK_GROUPS * S5_STATE
S5_SUB = 16


def _sigmoid(v):
    return 0.5 * (1.0 + jnp.tanh(0.5 * v))


def _silu(v):
    return v * _sigmoid(v)


def _dot(a, b):
    return jnp.dot(a, b, preferred_element_type=F32)


def _dot_nt(a, b):
    return lax.dot_general(a, b, (((1,), (1,)), ((), ())), preferred_element_type=F32)


def _dot_tn(a, b):
    return lax.dot_general(a, b, (((0,), (0,)), ((), ())), preferred_element_type=F32)


def _params(*sem):
    return pltpu.CompilerParams(dimension_semantics=sem, vmem_limit_bytes=VMEM_LIMIT)


def _ada_kernel(ct_ref, w_ref, b_ref, o_ref, sc_ref):
    nb = sc_ref.shape[0]
    d, tn = w_ref.shape

    @pl.when(pl.program_id(0) == 0)
    def _():
        sc = _silu(ct_ref[...])
        for b in range(nb):
            sc_ref[b] = jnp.broadcast_to(sc[:, b:b + 1], (d, LANES))

    def body(g, accs):
        r0 = pl.multiple_of(g * SUBLANES, SUBLANES)
        w = w_ref[pl.ds(r0, SUBLANES), :]
        return tuple(acc + w * jnp.concatenate([sc_ref[b, pl.ds(r0, SUBLANES), :]] * (tn // LANES), axis=1)
                     for b, acc in enumerate(accs))

    accs = lax.fori_loop(0, d // SUBLANES, body,
                         tuple(jnp.zeros((SUBLANES, tn), F32) for _ in range(nb)), unroll=8)
    for b in range(nb):
        o_ref[b:b + 1, :] = jnp.sum(accs[b], axis=0, keepdims=True) + b_ref[...]


def _ada(ct, w, b):
    d, n = w.shape
    nb = ct.shape[1]
    tn = 1024
    return pl.pallas_call(
        _ada_kernel,
        grid=(n // tn,),
        in_specs=[pl.BlockSpec((d, nb), lambda j: (0, 0)),
                  pl.BlockSpec((d, tn), lambda j: (0, j)),
                  pl.BlockSpec((1, tn), lambda j: (0, j))],
        out_specs=pl.BlockSpec((nb, tn), lambda j: (0, j)),
        out_shape=jax.ShapeDtypeStruct((nb, n), F32),
        scratch_shapes=[pltpu.VMEM((nb, d, LANES), F32)],
        compiler_params=_params("arbitrary"),
        name="ada",
    )(ct, w, b)


WPREP_ROWS = 1024


def _wprep_kernel(cur_ref, nxt_ref, o_ref, g_ref, *, n_plain):
    i = pl.program_id(0)
    keep = WPREP_ROWS - GLA_GATE_RANK

    @pl.when(i < n_plain)
    def _():
        o_ref[...] = cur_ref[...].astype(BF16)

    @pl.when(i >= n_plain)
    def _():
        o_ref[:keep] = cur_ref[GLA_GATE_RANK:].astype(BF16)
        o_ref[keep:] = nxt_ref[...].astype(BF16)

    @pl.when(i == n_plain - 1)
    def _():
        g_ref[:GLA_GATE_RANK] = nxt_ref[...].astype(BF16)
        g_ref[GLA_GATE_RANK:] = jnp.zeros((LANES - GLA_GATE_RANK, g_ref.shape[1]), BF16)


def _wprep(wt):
    n, d = wt.shape
    assert n == N_PROJ + GLA_GATE_RANK and COL_ZG % WPREP_ROWS == 0 and N_PROJ % WPREP_ROWS == 0
    per = WPREP_ROWS // GLA_GATE_RANK
    return pl.pallas_call(
        functools.partial(_wprep_kernel, n_plain=COL_ZG // WPREP_ROWS),
        grid=(N_PROJ // WPREP_ROWS,),
        in_specs=[pl.BlockSpec((WPREP_ROWS, d), lambda i: (i, 0)),
                  pl.BlockSpec((GLA_GATE_RANK, d), lambda i: ((i + 1) * per, 0))],
        out_specs=(pl.BlockSpec((WPREP_ROWS, d), lambda i: (i, 0)),
                   pl.BlockSpec((LANES, d), lambda i: (0, 0))),
        out_shape=(jax.ShapeDtypeStruct((N_PROJ, d), BF16),
                   jax.ShapeDtypeStruct((LANES, d), BF16)),
        compiler_params=_params("arbitrary"),
        name="wprep",
    )(wt, wt)


def _inproj_kernel(x_hbm, shift_ref, scale_ref, w_ref, wg_ref, o_ref, g_ref, h_ref, x_buf, x_sem, *, tm, nt, n_tiles):
    b = pl.program_id(0)
    i = pl.program_id(1)

    def x_copy(tile):
        r0 = pl.multiple_of((tile % nt) * tm, tm)
        return pltpu.make_async_copy(x_hbm.at[tile // nt, pl.ds(r0, tm), :], x_buf, x_sem)

    @pl.when(pl.program_id(2) == 0)
    def _():
        tile = b * nt + i

        @pl.when(tile == 0)
        def _():
            x_copy(tile).start()

        x_copy(tile).wait()
        h_ref[...] = (x_buf[...] * (1.0 + scale_ref[0]) + shift_ref[0]).astype(BF16)

        @pl.when(tile + 1 < n_tiles)
        def _():
            x_copy(tile + 1).start()

        g_ref[...] = _dot_nt(h_ref[...], wg_ref[...]).astype(g_ref.dtype)

    o_ref[...] = _dot_nt(h_ref[...], w_ref[...]).astype(o_ref.dtype)


def _inproj(x, mod3, w, wg):
    bsz, seq, d = x.shape
    n = w.shape[0]
    tn = INPROJ_TN
    tm = min(1024, seq)
    nt = seq // tm
    return pl.pallas_call(
        functools.partial(_inproj_kernel, tm=tm, nt=nt, n_tiles=bsz * nt),
        grid=(bsz, nt, n // tn),
        in_specs=[pl.BlockSpec(memory_space=pl.ANY),
                  pl.BlockSpec((1, 1, d), lambda b, i, j: (b, 0, 0)),
                  pl.BlockSpec((1, 1, d), lambda b, i, j: (b, 0, 1)),
                  pl.BlockSpec((tn, d), lambda b, i, j: (j, 0)),
                  pl.BlockSpec((LANES, d), lambda b, i, j: (0, 0))],
        out_specs=(pl.BlockSpec((tm, tn), lambda b, i, j: (b * nt + i, j)),
                   pl.BlockSpec((tm, LANES), lambda b, i, j: (b * nt + i, 0))),
        out_shape=(jax.ShapeDtypeStruct((bsz * seq, n), BF16),
                   jax.ShapeDtypeStruct((bsz * seq, LANES), BF16)),
        scratch_shapes=[pltpu.VMEM((tm, d), BF16), pltpu.VMEM((tm, d), F32), pltpu.SemaphoreType.DMA(())],
        compiler_params=_params("arbitrary", "arbitrary", "arbitrary"),
        name="inproj",
    )(x, mod3, mod3, w, wg)


GLA_BLOCK = 4 * GLA_CHUNK
GLA_NCHUNK = GLA_BLOCK // GLA_CHUNK
GLA_STEP_BLOCKS = 4


def _gla_constants():
    import numpy as np
    tc = GLA_BLOCK
    row = np.arange(tc)[:, None]
    col = np.arange(tc)[None, :]
    rc = row // GLA_CHUNK
    cc = col // GLA_CHUNK
    half = GLA_NCHUNK // 2
    same = rc == cc
    totals = np.arange(2 * SUBLANES)[:, None] == cc
    sums = np.concatenate([same & (col <= row), totals], axis=0)
    sel = np.where(same & (col <= row), 1.0,
                   np.where((rc == cc + 1) & (rc != half), 2.0,
                            np.where((rc >= half) & (cc < half), 3.0, 0.0)))
    return jnp.asarray(sums.astype(np.float32), dtype=BF16), jnp.asarray(sel.astype(np.float32))


def _gla_kernel(q_ref, k_ref, g_ref, v_ref, z_ref, wg_ref, bg_ref, ng_ref, sums_ref, sel_ref, o_ref, st_ref):
    @pl.when(pl.program_id(1) == 0)
    def _():
        st_ref[...] = jnp.zeros_like(st_ref)

    tc = GLA_BLOCK
    sel = sel_ref[...]
    logit = _dot(g_ref[...], wg_ref[...]) + bg_ref[...]
    la_all = (jnp.minimum(logit, 0.0) - jnp.log(1.0 + jnp.exp(-jnp.abs(logit)))) * (1.0 / GLA_GATE_TAU)
    for blk, hd in itertools.product(range(GLA_STEP_BLOCKS), range(GLA_HEADS)):
        rs = slice(blk * tc, (blk + 1) * tc)
        ck = slice(hd * GLA_DK, (hd + 1) * GLA_DK)
        cv = slice(hd * GLA_DV, (hd + 1) * GLA_DV)
        la = la_all[rs, ck]
        la_hi = la.astype(BF16)
        la_lo = (la - la_hi.astype(F32)).astype(BF16)
        sums = _dot(sums_ref[...], la_hi) + _dot(sums_ref[...], la_lo)
        b = sums[:tc]
        tot = [sums[tc + c:tc + c + 1] for c in range(GLA_NCHUNK)]
        half = GLA_NCHUNK // 2

        def span(lo, hi):
            return sum(tot[lo:hi]) if hi > lo else jnp.zeros_like(tot[0])

        k = k_ref[rs, ck].astype(F32)
        q_dec = q_ref[rs, ck].astype(F32) * (GLA_DK ** -0.5) * jnp.exp(b)
        k_inv = k * jnp.exp(-b)
        k_end, q_in, k_st, q_mid, k_mid = [], [], [], [], []
        for c in range(GLA_NCHUNK):
            rows = slice(c * GLA_CHUNK, (c + 1) * GLA_CHUNK)
            ke = k[rows] * jnp.exp(tot[c] - b[rows])
            e_mid = jnp.exp(span(half, c) if c >= half else span(c + 1, half))
            k_end.append(ke)
            q_in.append(q_dec[rows] * jnp.exp(span(0, c)))
            k_st.append(ke * jnp.exp(span(c + 1, GLA_NCHUNK)))
            q_mid.append(q_dec[rows] * e_mid)
            k_mid.append(ke * e_mid)
        cat = lambda parts: jnp.concatenate(parts, axis=0).astype(BF16)
        a_same = _dot_nt(q_dec.astype(BF16), k_inv.astype(BF16))
        a_next = _dot_nt(q_dec.astype(BF16), cat(k_end))
        a_mid = _dot_nt(cat(q_mid), cat(k_mid))
        att = jnp.where(sel == 1.0, a_same,
                        jnp.where(sel == 2.0, a_next, jnp.where(sel == 3.0, a_mid, 0.0))).astype(BF16)
        v = v_ref[rs, cv]
        st = st_ref[hd]
        o = _dot(att, v) + _dot_nt(cat(q_in), st.astype(BF16))
        st_ref[hd] = st * jnp.exp(span(0, GLA_NCHUNK)) + _dot_tn(v, cat(k_st))
        o = o * lax.rsqrt(jnp.mean(o * o, axis=-1, keepdims=True) + NORM_EPS) * ng_ref[...]
        z = z_ref[rs, cv].astype(F32)
        o_ref[rs, cv] = (o * _silu(z)).astype(o_ref.dtype)


def _gla(proj, g_lr, wg, bg, ng, bsz, seq):
    tc = GLA_BLOCK * GLA_STEP_BLOCKS
    nt = seq // tc
    row = lambda b, i: b * nt + i
    sums, sel = _gla_constants()
    const = lambda b, i: (0, 0)
    return pl.pallas_call(
        _gla_kernel,
        grid=(bsz, nt),
        in_specs=[pl.BlockSpec((tc, DK_TOT), lambda b, i: (row(b, i), COL_Q // DK_TOT)),
                  pl.BlockSpec((tc, DK_TOT), lambda b, i: (row(b, i), COL_K // DK_TOT)),
                  pl.BlockSpec((tc, LANES), lambda b, i: (row(b, i), 0)),
                  pl.BlockSpec((tc, D_GLA), lambda b, i: (row(b, i), COL_V // D_GLA)),
                  pl.BlockSpec((tc, D_GLA), lambda b, i: (row(b, i), COL_ZG // D_GLA)),
                  pl.BlockSpec((LANES, DK_TOT), const),
                  pl.BlockSpec((1, DK_TOT), const),
                  pl.BlockSpec((1, GLA_DV), const),
                  pl.BlockSpec((GLA_BLOCK + 2 * SUBLANES, GLA_BLOCK), const),
                  pl.BlockSpec((GLA_BLOCK, GLA_BLOCK), const)],
        out_specs=pl.BlockSpec((tc, D_GLA), lambda b, i: (row(b, i), 0)),
        out_shape=jax.ShapeDtypeStruct((bsz * seq, D_GLA), BF16),
        scratch_shapes=[pltpu.VMEM((GLA_HEADS, GLA_DV, GLA_DK), F32)],
        compiler_params=_params("parallel", "arbitrary"),
        name="gla",
    )(proj, proj, g_lr, proj, proj, wg, bg, ng, sums, sel)


def _s5prep_kernel(lre_ref, lim_ref, ldt_ref, bre_ref, bim_ref, cre_ref, cim_ref,
                   are_ref, aim_ref, wb_ref, wc_ref):
    lre = lre_ref[...]
    lim = lim_ref[...]
    dt = jnp.exp(ldt_ref[...])
    z_re = lre * dt
    z_im = lim * dt
    mag = jnp.exp(z_re)
    ab_re = mag * jnp.cos(z_im)
    ab_im = mag * jnp.sin(z_im)
    den = lre * lre + lim * lim
    n_re = ab_re - 1.0
    n_im = ab_im
    f_re = (n_re * lre + n_im * lim) / den
    f_im = (n_im * lre - n_re * lim) / den
    b_re = bre_ref[...]
    b_im = bim_ref[...]
    are_ref[...] = ab_re
    aim_ref[...] = ab_im
    g, hh, p = b_re.shape
    rows2d = lambda t: t.reshape(g * hh, p).astype(BF16)
    bb = (rows2d(f_re * b_re - f_im * b_im), rows2d(f_re * b_im + f_im * b_re))
    cc = (rows2d(cre_ref[...]), rows2d(-cim_ref[...]))

    nsb = wb_ref.shape[0]
    blk_rows = S5_BLOCK_GROUPS * hh
    ri = lax.broadcasted_iota(jnp.int32, (p, S5_BLOCK_S), 0)
    ci = lax.broadcasted_iota(jnp.int32, (p, S5_BLOCK_S), 1)
    spread = jnp.where(ci % p == ri, 1.0, 0.0).astype(BF16)
    ri = lax.broadcasted_iota(jnp.int32, (S5_BLOCK_S, p), 0)
    ci = lax.broadcasted_iota(jnp.int32, (S5_BLOCK_S, p), 1)
    spread_t = jnp.where(ri % p == ci, 1.0, 0.0).astype(BF16)
    ri = lax.broadcasted_iota(jnp.int32, (blk_rows, S5_BLOCK_S), 0)
    ci = lax.broadcasted_iota(jnp.int32, (blk_rows, S5_BLOCK_S), 1)
    diag_b = (ri // hh) == (ci // p)
    ri = lax.broadcasted_iota(jnp.int32, (S5_BLOCK_S, blk_rows), 0)
    ci = lax.broadcasted_iota(jnp.int32, (S5_BLOCK_S, blk_rows), 1)
    diag_c = (ri // p) == (ci // hh)
    for sb in range(nsb):
        for hf in range(2):
            r0 = (hf * (g // 2) + sb * S5_BLOCK_GROUPS) * hh
            for part in range(2):
                y = jnp.where(diag_b, _dot(bb[part][r0:r0 + blk_rows], spread), 0.0)
                wb_ref[sb, hf * blk_rows:(hf + 1) * blk_rows,
                       part * S5_BLOCK_S:(part + 1) * S5_BLOCK_S] = y.astype(BF16)
                z = jnp.where(diag_c, _dot_nt(spread_t, cc[part][r0:r0 + blk_rows]), 0.0)
                wc_ref[sb, part * S5_BLOCK_S:(part + 1) * S5_BLOCK_S,
                       hf * blk_rows:(hf + 1) * blk_rows] = z.astype(BF16)


def _s5prep(lam_re, lam_im, log_dt, bt_re, bt_im, c_re, c_im):
    g, p = lam_re.shape
    nsb = g // (2 * S5_BLOCK_GROUPS)
    a_shape = jax.ShapeDtypeStruct((g, 1, p), F32)
    return pl.pallas_call(
        _s5prep_kernel,
        out_shape=(a_shape, a_shape,
                   jax.ShapeDtypeStruct((nsb, 2 * S5_BLOCK_U, 2 * S5_BLOCK_S), BF16),
                   jax.ShapeDtypeStruct((nsb, 2 * S5_BLOCK_S, 2 * S5_BLOCK_U), BF16)),
        compiler_params=pltpu.CompilerParams(vmem_limit_bytes=VMEM_LIMIT),
        name="s5prep",
    )(lam_re.reshape(g, 1, p), lam_im.reshape(g, 1, p), log_dt.reshape(g, 1, 1), bt_re, bt_im, c_re, c_im)


def _s5_kernel(u0_ref, u1_ref, wb_ref, wc_ref, a_ref, d_ref, o_ref, uf_ref, bu_ref, s_ref, y_ref, carry_ref, *, tc):
    @pl.when(pl.program_id(1) == 0)
    def _():
        carry_ref[...] = jnp.zeros_like(carry_ref)

    nb = u0_ref.shape[0]
    tile = SUBLANES * S5_SUB
    for b in range(nb):
        uf_ref[pl.ds(b, tc, stride=SUBLANES), :] = u0_ref[b].astype(F32)
        uf_ref[pl.ds(nb + b, tc, stride=SUBLANES), :] = u1_ref[b].astype(F32)

    lo = (lax.broadcasted_iota(jnp.int32, (tile, S5_BLOCK_U), 0) & nb) == 0
    a_re = a_ref[0, :, :S5_BLOCK_S]
    a_im = a_ref[0, :, S5_BLOCK_S:]
    s_re = carry_ref[:, :S5_BLOCK_S]
    s_im = carry_ref[:, S5_BLOCK_S:]

    for s in range(tc // S5_SUB):
        base = s * tile
        uf = uf_ref[base:base + tile, :]
        lhs = jnp.concatenate([jnp.where(lo, uf, 0.0), jnp.where(lo, 0.0, uf)], axis=1).astype(BF16)
        bu_ref[base:base + tile, :] = _dot(lhs, wb_ref[0])
        for p in range(S5_SUB // 2):
            r0 = base + p * 2 * SUBLANES
            r1 = r0 + SUBLANES
            m_re = a_re * s_re - a_im * s_im + bu_ref[r0:r1, :S5_BLOCK_S]
            m_im = a_re * s_im + a_im * s_re + bu_ref[r0:r1, S5_BLOCK_S:]
            s_re = a_re * m_re - a_im * m_im + bu_ref[r1:r1 + SUBLANES, :S5_BLOCK_S]
            s_im = a_re * m_im + a_im * m_re + bu_ref[r1:r1 + SUBLANES, S5_BLOCK_S:]
            s_ref[r0:r0 + 2 * SUBLANES, :S5_BLOCK_S] = jnp.concatenate([m_re, s_re], axis=0).astype(BF16)
            s_ref[r0:r0 + 2 * SUBLANES, S5_BLOCK_S:] = jnp.concatenate([m_im, s_im], axis=0).astype(BF16)
        y8 = _dot(s_ref[base:base + tile, :], wc_ref[0])
        y = jnp.where(lo, y8[:, :S5_BLOCK_U], y8[:, S5_BLOCK_U:])
        y = (y.reshape(S5_SUB, SUBLANES, S5_BLOCK_U)
             + d_ref[...] * uf.reshape(S5_SUB, SUBLANES, S5_BLOCK_U)).reshape(tile, S5_BLOCK_U)
        cdf = 0.5 * (1.0 + jnp.tanh(math.sqrt(2.0 / math.pi) * (y + 0.044715 * (y * y * y))))
        y_ref[base:base + tile, :] = y * cdf

    carry_ref[:, :S5_BLOCK_S] = s_re
    carry_ref[:, S5_BLOCK_S:] = s_im
    for j in range(2 * nb):
        o_ref[j] = y_ref[pl.ds(j, tc, stride=SUBLANES), :].astype(o_ref.dtype)


def _s5(proj3, wb, wc, a8, d8):
    bsz, seq, _ = proj3.shape
    nsb = wb.shape[0]
    tc = min(512, seq)
    rows = SUBLANES * tc
    cb = COL_U // S5_BLOCK_U
    return pl.pallas_call(
        functools.partial(_s5_kernel, tc=tc),
        grid=(nsb, seq // tc),
        in_specs=[pl.BlockSpec((bsz, tc, S5_BLOCK_U), lambda sb, i: (0, i, cb + sb)),
                  pl.BlockSpec((bsz, tc, S5_BLOCK_U), lambda sb, i: (0, i, cb + nsb + sb)),
                  pl.BlockSpec((1, 2 * S5_BLOCK_U, 2 * S5_BLOCK_S), lambda sb, i: (sb, 0, 0)),
                  pl.BlockSpec((1, 2 * S5_BLOCK_S, 2 * S5_BLOCK_U), lambda sb, i: (sb, 0, 0)),
                  pl.BlockSpec((1, SUBLANES, 2 * S5_BLOCK_S), lambda sb, i: (sb, 0, 0)),
                  pl.BlockSpec((1, SUBLANES, S5_BLOCK_U), lambda sb, i: (sb, 0, 0))],
        out_specs=pl.BlockSpec((2 * bsz, tc, S5_BLOCK_U), lambda sb, i: (0, i, sb)),
        out_shape=jax.ShapeDtypeStruct((2 * bsz, seq, nsb * S5_BLOCK_U), BF16),
        scratch_shapes=[pltpu.VMEM((rows, S5_BLOCK_U), F32),
                        pltpu.VMEM((rows, 2 * S5_BLOCK_S), F32),
                        pltpu.VMEM((rows, 2 * S5_BLOCK_S), BF16),
                        pltpu.VMEM((rows, S5_BLOCK_U), F32),
                        pltpu.VMEM((SUBLANES, 2 * S5_BLOCK_S), F32)],
        compiler_params=_params("parallel", "arbitrary"),
        name="s5",
    )(proj3, proj3, wb, wc, a8, d8)


def _glu_kernel(y0_ref, y1_ref, z_ref, w_ref, b_ref, o_ref):
    yb = jnp.concatenate([y0_ref[0], y1_ref[0]], axis=1)
    acc = _dot(yb, w_ref[...]) + b_ref[...]
    y = yb.astype(F32)
    z = z_ref[...].astype(F32)
    o_ref[...] = (y * z * (0.25 * (1.0 + jnp.tanh(0.5 * acc)) * (1.0 + jnp.tanh(0.5 * z)))).astype(o_ref.dtype)


def _glu(yv, proj, w, b, bsz, seq):
    d = w.shape[0]
    half = d // 2
    tm = min(1024, seq)
    nt = seq // tm
    return pl.pallas_call(
        _glu_kernel,
        grid=(bsz, nt),
        in_specs=[pl.BlockSpec((1, tm, half), lambda b_, i: (b_, i, 0)),
                  pl.BlockSpec((1, tm, half), lambda b_, i: (bsz + b_, i, 0)),
                  pl.BlockSpec((tm, d), lambda b_, i: (b_ * nt + i, COL_ZS // D_S5)),
                  pl.BlockSpec((d, d), lambda b_, i: (0, 0)),
                  pl.BlockSpec((1, d), lambda b_, i: (0, 0))],
        out_specs=pl.BlockSpec((tm, d), lambda b_, i: (b_ * nt + i, 0)),
        out_shape=jax.ShapeDtypeStruct((bsz * seq, d), BF16),
        compiler_params=_params("parallel", "parallel"),
        name="glu",
    )(yv, yv, proj, w, b)


OUT_EPILOGUE_CHUNKS = 2


def _out_kernel(og_ref, os_ref, w_ref, x_hbm, gate_ref, lg_ref, lb_ref, o_ref, x_buf, x_sem, *, nk, tm):
    b = pl.program_id(0)
    i = pl.program_id(1)
    k = pl.program_id(2)
    half = nk // 2

    def x_copy():
        r0 = pl.multiple_of(i * tm, tm)
        return pltpu.make_async_copy(x_hbm.at[b, pl.ds(r0, tm), :], x_buf, x_sem)

    @pl.when(k == min(1, nk - 1))
    def _():
        x_copy().start()

    @pl.when(k == 0)
    def _():
        o_ref[0] = _dot(og_ref[...], w_ref[...])

    @pl.when((k > 0) & (k < half))
    def _():
        o_ref[0] += _dot(og_ref[...], w_ref[...])

    @pl.when((k >= half) & (k < nk - 1))
    def _():
        o_ref[0] += _dot(os_ref[...], w_ref[...])

    @pl.when(k == nk - 1)
    def _():
        x_copy().wait()
        rows = tm // OUT_EPILOGUE_CHUNKS
        gate = gate_ref[0] * (1.0 / DEEPNORM_ALPHA)
        for c in range(OUT_EPILOGUE_CHUNKS):
            rs = slice(c * rows, (c + 1) * rows)
            mixed = o_ref[0, rs, :] + _dot(os_ref[rs, :], w_ref[...])
            r = x_buf[rs, :] + gate * mixed
            mu = jnp.mean(r, axis=-1, keepdims=True)
            rc = r - mu
            var = jnp.mean(rc * rc, axis=-1, keepdims=True)
            o_ref[0, rs, :] = (rc * lax.rsqrt(var + NORM_EPS / DEEPNORM_ALPHA ** 2) * lg_ref[...]
                               + lb_ref[...])


def _outproj(og, osb, w, x, mod3, lg, lb):
    bsz, seq, d = x.shape
    dh = og.shape[1]
    tm = min(512, seq)
    nt = seq // tm
    tk = 1024
    half = dh // tk
    nk = 2 * half
    return pl.pallas_call(
        functools.partial(_out_kernel, nk=nk, tm=tm),
        grid=(bsz, nt, nk),
        in_specs=[pl.BlockSpec((tm, tk), lambda b, i, k: (b * nt + i, jnp.minimum(k, half - 1))),
                  pl.BlockSpec((tm, tk), lambda b, i, k: (b * nt + i, jnp.maximum(k - half, 0))),
                  pl.BlockSpec((tk, d), lambda b, i, k: (k, 0)),
                  pl.BlockSpec(memory_space=pl.ANY),
                  pl.BlockSpec((1, 1, d), lambda b, i, k: (b, 0, 2)),
                  pl.BlockSpec((1, d), lambda b, i, k: (0, 0)),
                  pl.BlockSpec((1, d), lambda b, i, k: (0, 0))],
        out_specs=pl.BlockSpec((1, tm, d), lambda b, i, k: (b, i, 0)),
        out_shape=jax.ShapeDtypeStruct(x.shape, x.dtype),
        scratch_shapes=[pltpu.VMEM((tm, d), F32), pltpu.SemaphoreType.DMA(())],
        compiler_params=_params("parallel", "parallel", "arbitrary"),
        name="outproj",
    )(og, osb, w, x, mod3, lg, lb)


def _s5_layouts(ab_re, ab_im, d_skip, bsz):
    g = ab_re.shape[0]
    nsb = g // (2 * S5_BLOCK_GROUPS)

    def rows8(v, width):
        t = v.reshape(2, nsb, 1, width).transpose(1, 0, 2, 3)
        return jnp.broadcast_to(t, (nsb, 2, bsz, width)).reshape(nsb, 2 * bsz, width)

    a8 = jnp.concatenate([rows8(ab_re.reshape(-1), S5_BLOCK_S), rows8(ab_im.reshape(-1), S5_BLOCK_S)], axis=2)
    d8 = rows8(d_skip, S5_BLOCK_U)
    return a8, d8


def kernel(x, c, w_ada, b_ada, w_in, w_gla_gate, b_gla_gate, gla_norm_g, s5_lambda_re, s5_lambda_im, s5_log_dt, s5_b_re, s5_b_im, s5_c_re, s5_c_im, s5_d, w_glu, b_glu, w_out, ln_g, ln_b):
    bsz, seq, d = x.shape
    assert bsz * 2 == SUBLANES and w_ada.shape[0] == DEPTH and s5_d.shape[1] == D_S5
    layer = 0

    mod = _ada(c.T, w_ada[layer], b_ada[layer][None, :])
    mod3 = mod.reshape(bsz, 1, 3 * d)

    w_main, w_glr = _wprep(jnp.swapaxes(w_in, 1, 2)[layer])
    proj, g_lr = _inproj(x, mod3, w_main, w_glr)

    wg = jnp.pad(w_gla_gate[layer], ((0, LANES - GLA_GATE_RANK), (0, 0))).astype(BF16)
    o_gla = _gla(proj, g_lr, wg, b_gla_gate[layer][None, :], gla_norm_g[layer][None, :], bsz, seq)

    bt_re = s5_b_re[layer].transpose(0, 2, 1)
    bt_im = s5_b_im[layer].transpose(0, 2, 1)
    ab_re, ab_im, wb, wc = _s5prep(s5_lambda_re[layer], s5_lambda_im[layer], s5_log_dt[layer],
                                   bt_re, bt_im, s5_c_re[layer], s5_c_im[layer])
    a8, d8 = _s5_layouts(ab_re, ab_im, s5_d[layer], bsz)
    yv = _s5(proj.reshape(bsz, seq, N_PROJ), wb, wc, a8, d8)
    o_s5 = _glu(yv, proj, w_glu[layer].astype(BF16), b_glu[layer][None, :], bsz, seq)

    return _outproj(o_gla, o_s5, w_out[layer].astype(BF16), x, mod3,
                    ln_g[layer][None, :], ln_b[layer][None, :])
```

```python
import functools
import itertools
import math

import jax
import jax.numpy as jnp
from jax import lax
from jax.experimental import pallas as pl
from jax.experimental.pallas import tpu as pltpu

F32 = jnp.float32
BF16 = jnp.bfloat16

GLA_HEADS = 4
GLA_DK = 256
GLA_DV = 512
GLA_GATE_RANK = 16
GLA_GATE_TAU = 16.0
GLA_CHUNK = 64
S5_GROUP = 16
S5_STATE = 64
NORM_EPS = 1e-5
DEPTH = 1
DEEPNORM_ALPHA = (2.0 * DEPTH) ** 0.25

LANES = 128
SUBLANES = 8
VMEM_LIMIT = 60 * 1024 * 1024

DK_TOT = GLA_HEADS * GLA_DK
D_GLA = GLA_HEADS * GLA_DV
D_S5 = 2048
COL_Q = 0
COL_K = COL_Q + DK_TOT
COL_V = COL_K + DK_TOT
COL_ZG = COL_V + D_GLA
COL_U = COL_ZG + D_GLA
COL_ZS = COL_U + D_S5
N_PROJ = COL_ZS + D_S5
MXU_WIDTH = 256
INPROJ_TN = 5 * MXU_WIDTH

S5_BLOCK_GROUPS = 8
S5_BLOCK_U = S5_BLOCK_GROUPS * S5_GROUP
S5_BLOCK_S = S5_BLOCK_GROUPS * S5_STATE
S5_SUB = 16


def _sigmoid(v):
    return 0.5 * (1.0 + jnp.tanh(0.5 * v))


def _silu(v):
    return v * _sigmoid(v)


def _dot(a, b):
    return jnp.dot(a, b, preferred_element_type=F32)


def _dot_nt(a, b):
    return lax.dot_general(a, b, (((1,), (1,)), ((), ())), preferred_element_type=F32)


def _dot_tn(a, b):
    return lax.dot_general(a, b, (((0,), (0,)), ((), ())), preferred_element_type=F32)


def _params(*sem):
    return pltpu.CompilerParams(dimension_semantics=sem, vmem_limit_bytes=VMEM_LIMIT)


def _ada_kernel(ct_ref, w_ref, b_ref, o_ref, sc_ref):
    nb = sc_ref.shape[0]
    d, tn = w_ref.shape

    @pl.when(pl.program_id(0) == 0)
    def _():
        sc = _silu(ct_ref[...])
        for b in range(nb):
            sc_ref[b] = jnp.broadcast_to(sc[:, b:b + 1], (d, LANES))

    def body(g, accs):
        r0 = pl.multiple_of(g * SUBLANES, SUBLANES)
        w = w_ref[pl.ds(r0, SUBLANES), :]
        return tuple(acc + w * jnp.concatenate([sc_ref[b, pl.ds(r0, SUBLANES), :]] * (tn // LANES), axis=1)
                     for b, acc in enumerate(accs))

    accs = lax.fori_loop(0, d // SUBLANES, body,
                         tuple(jnp.zeros((SUBLANES, tn), F32) for _ in range(nb)), unroll=8)
    for b in range(nb):
        o_ref[b:b + 1, :] = jnp.sum(accs[b], axis=0, keepdims=True) + b_ref[...]


def _ada(ct, w, b):
    d, n = w.shape
    nb = ct.shape[1]
    tn = 1024
    return pl.pallas_call(
        _ada_kernel,
        grid=(n // tn,),
        in_specs=[pl.BlockSpec((d, nb), lambda j: (0, 0)),
                  pl.BlockSpec((d, tn), lambda j: (0, j)),
                  pl.BlockSpec((1, tn), lambda j: (0, j))],
        out_specs=pl.BlockSpec((nb, tn), lambda j: (0, j)),
        out_shape=jax.ShapeDtypeStruct((nb, n), F32),
        scratch_shapes=[pltpu.VMEM((nb, d, LANES), F32)],
        compiler_params=_params("arbitrary"),
        name="ada",
    )(ct, w, b)


WPREP_ROWS = 512


def _wprep_kernel(cur_ref, nxt_ref, o_ref, g_ref, *, n_plain):
    i = pl.program_id(0)
    keep = WPREP_ROWS - GLA_GATE_RANK

    @pl.when(i < n_plain)
    def _():
        o_ref[...] = cur_ref[...].astype(BF16)

    @pl.when(i >= n_plain)
    def _():
        o_ref[:keep] = cur_ref[GLA_GATE_RANK:].astype(BF16)
        o_ref[keep:] = nxt_ref[...].astype(BF16)

    @pl.when(i == n_plain - 1)
    def _():
        g_ref[:GLA_GATE_RANK] = nxt_ref[...].astype(BF16)
        g_ref[GLA_GATE_RANK:] = jnp.zeros((LANES - GLA_GATE_RANK, g_ref.shape[1]), BF16)


def _wprep(wt):
    n, d = wt.shape
    assert n == N_PROJ + GLA_GATE_RANK and COL_ZG % WPREP_ROWS == 0 and N_PROJ % WPREP_ROWS == 0
    per = WPREP_ROWS // GLA_GATE_RANK
    return pl.pallas_call(
        functools.partial(_wprep_kernel, n_plain=COL_ZG // WPREP_ROWS),
        grid=(N_PROJ // WPREP_ROWS,),
        in_specs=[pl.BlockSpec((WPREP_ROWS, d), lambda i: (i, 0)),
                  pl.BlockSpec((GLA_GATE_RANK, d), lambda i: ((i + 1) * per, 0))],
        out_specs=(pl.BlockSpec((WPREP_ROWS, d), lambda i: (i, 0)),
                   pl.BlockSpec((LANES, d), lambda i: (0, 0))),
        out_shape=(jax.ShapeDtypeStruct((N_PROJ, d), BF16),
                   jax.ShapeDtypeStruct((LANES, d), BF16)),
        compiler_params=_params("arbitrary"),
        name="wprep",
    )(wt, wt)


def _inproj_kernel(x_hbm, shift_ref, scale_ref, w_ref, wg_ref, o_ref, g_ref, h_ref, x_buf, x_sem, *, tm, nt, n_tiles):
    b = pl.program_id(0)
    i = pl.program_id(1)

    def x_copy(tile):
        r0 = pl.multiple_of((tile % nt) * tm, tm)
        return pltpu.make_async_copy(x_hbm.at[tile // nt, pl.ds(r0, tm), :], x_buf, x_sem)

    @pl.when(pl.program_id(2) == 0)
    def _():
        tile = b * nt + i

        @pl.when(tile == 0)
        def _():
            x_copy(tile).start()

        x_copy(tile).wait()
        h_ref[...] = (x_buf[...] * (1.0 + scale_ref[0]) + shift_ref[0]).astype(BF16)

        @pl.when(tile + 1 < n_tiles)
        def _():
            x_copy(tile + 1).start()

        g_ref[...] = _dot_nt(h_ref[...], wg_ref[...]).astype(g_ref.dtype)

    o_ref[...] = _dot_nt(h_ref[...], w_ref[...]).astype(o_ref.dtype)


def _inproj(x, mod3, w, wg):
    bsz, seq, d = x.shape
    n = w.shape[0]
    tn = INPROJ_TN
    tm = min(1024, seq)
    nt = seq // tm
    return pl.pallas_call(
        functools.partial(_inproj_kernel, tm=tm, nt=nt, n_tiles=bsz * nt),
        grid=(bsz, nt, n // tn),
        in_specs=[pl.BlockSpec(memory_space=pl.ANY),
                  pl.BlockSpec((1, 1, d), lambda b, i, j: (b, 0, 0)),
                  pl.BlockSpec((1, 1, d), lambda b, i, j: (b, 0, 1)),
                  pl.BlockSpec((tn, d), lambda b, i, j: (j, 0)),
                  pl.BlockSpec((LANES, d), lambda b, i, j: (0, 0))],
        out_specs=(pl.BlockSpec((tm, tn), lambda b, i, j: (b * nt + i, j)),
                   pl.BlockSpec((tm, LANES), lambda b, i, j: (b * nt + i, 0))),
        out_shape=(jax.ShapeDtypeStruct((bsz * seq, n), BF16),
                   jax.ShapeDtypeStruct((bsz * seq, LANES), BF16)),
        scratch_shapes=[pltpu.VMEM((tm, d), BF16), pltpu.VMEM((tm, d), F32), pltpu.SemaphoreType.DMA(())],
        compiler_params=_params("arbitrary", "arbitrary", "arbitrary"),
        name="inproj",
    )(x, mod3, mod3, w, wg)


GLA_BLOCK = 4 * GLA_CHUNK
GLA_NCHUNK = GLA_BLOCK // GLA_CHUNK
GLA_STEP_BLOCKS = 4


def _gla_constants():
    import numpy as np
    tc = GLA_BLOCK
    row = np.arange(tc)[:, None]
    col = np.arange(tc)[None, :]
    rc = row // GLA_CHUNK
    cc = col // GLA_CHUNK
    half = GLA_NCHUNK // 2
    same = rc == cc
    totals = np.arange(2 * SUBLANES)[:, None] == cc
    sums = np.concatenate([same & (col <= row), totals], axis=0)
    sel = np.where(same & (col <= row), 1.0,
                   np.where((rc == cc + 1) & (rc != half), 2.0,
                            np.where((rc >= half) & (cc < half), 3.0, 0.0)))
    return jnp.asarray(sums.astype(np.float32), dtype=BF16), jnp.asarray(sel.astype(np.float32))


def _gla_kernel(q_ref, k_ref, g_ref, v_ref, z_ref, wg_ref, bg_ref, ng_ref, sums_ref, sel_ref, o_ref, st_ref):
    @pl.when(pl.program_id(1) == 0)
    def _():
        st_ref[...] = jnp.zeros_like(st_ref)

    tc = GLA_BLOCK
    sel = sel_ref[...]
    logit = _dot(g_ref[...], wg_ref[...]) + bg_ref[...]
    la_all = (jnp.minimum(logit, 0.0) - jnp.log(1.0 + jnp.exp(-jnp.abs(logit)))) * (1.0 / GLA_GATE_TAU)
    for blk, hd in itertools.product(range(GLA_STEP_BLOCKS), range(GLA_HEADS)):
        rs = slice(blk * tc, (blk + 1) * tc)
        ck = slice(hd * GLA_DK, (hd + 1) * GLA_DK)
        cv = slice(hd * GLA_DV, (hd + 1) * GLA_DV)
        la = la_all[rs, ck]
        la_hi = la.astype(BF16)
        la_lo = (la - la_hi.astype(F32)).astype(BF16)
        sums = _dot(sums_ref[...], la_hi) + _dot(sums_ref[...], la_lo)
        b = sums[:tc]
        tot = [sums[tc + c:tc + c + 1] for c in range(GLA_NCHUNK)]
        half = GLA_NCHUNK // 2

        def span(lo, hi):
            return sum(tot[lo:hi]) if hi > lo else jnp.zeros_like(tot[0])

        k = k_ref[rs, ck].astype(F32)
        q_dec = q_ref[rs, ck].astype(F32) * (GLA_DK ** -0.5) * jnp.exp(b)
        k_inv = k * jnp.exp(-b)
        k_end, q_in, k_st, q_mid, k_mid = [], [], [], [], []
        for c in range(GLA_NCHUNK):
            rows = slice(c * GLA_CHUNK, (c + 1) * GLA_CHUNK)
            ke = k[rows] * jnp.exp(tot[c] - b[rows])
            e_mid = jnp.exp(span(half, c) if c >= half else span(c + 1, half))
            k_end.append(ke)
            q_in.append(q_dec[rows] * jnp.exp(span(0, c)))
            k_st.append(ke * jnp.exp(span(c + 1, GLA_NCHUNK)))
            q_mid.append(q_dec[rows] * e_mid)
            k_mid.append(ke * e_mid)
        cat = lambda parts: jnp.concatenate(parts, axis=0).astype(BF16)
        a_same = _dot_nt(q_dec.astype(BF16), k_inv.astype(BF16))
        a_next = _dot_nt(q_dec.astype(BF16), cat(k_end))
        a_mid = _dot_nt(cat(q_mid), cat(k_mid))
        att = jnp.where(sel == 1.0, a_same,
                        jnp.where(sel == 2.0, a_next, jnp.where(sel == 3.0, a_mid, 0.0))).astype(BF16)
        v = v_ref[rs, cv]
        st = st_ref[hd]
        o = _dot(att, v) + _dot_nt(cat(q_in), st.astype(BF16))
        st_ref[hd] = st * jnp.exp(span(0, GLA_NCHUNK)) + _dot_tn(v, cat(k_st))
        o = o * lax.rsqrt(jnp.mean(o * o, axis=-1, keepdims=True) + NORM_EPS) * ng_ref[...]
        z = z_ref[rs, cv].astype(F32)
        o_ref[rs, cv] = (o * _silu(z)).astype(o_ref.dtype)


def _gla(proj, g_lr, wg, bg, ng, bsz, seq):
    tc = GLA_BLOCK * GLA_STEP_BLOCKS
    nt = seq // tc
    row = lambda b, i: b * nt + i
    sums, sel = _gla_constants()
    const = lambda b, i: (0, 0)
    return pl.pallas_call(
        _gla_kernel,
        grid=(bsz, nt),
        in_specs=[pl.BlockSpec((tc, DK_TOT), lambda b, i: (row(b, i), COL_Q // DK_TOT)),
                  pl.BlockSpec((tc, DK_TOT), lambda b, i: (row(b, i), COL_K // DK_TOT)),
                  pl.BlockSpec((tc, LANES), lambda b, i: (row(b, i), 0)),
                  pl.BlockSpec((tc, D_GLA), lambda b, i: (row(b, i), COL_V // D_GLA)),
                  pl.BlockSpec((tc, D_GLA), lambda b, i: (row(b, i), COL_ZG // D_GLA)),
                  pl.BlockSpec((LANES, DK_TOT), const),
                  pl.BlockSpec((1, DK_TOT), const),
                  pl.BlockSpec((1, GLA_DV), const),
                  pl.BlockSpec((GLA_BLOCK + 2 * SUBLANES, GLA_BLOCK), const),
                  pl.BlockSpec((GLA_BLOCK, GLA_BLOCK), const)],
        out_specs=pl.BlockSpec((tc, D_GLA), lambda b, i: (row(b, i), 0)),
        out_shape=jax.ShapeDtypeStruct((bsz * seq, D_GLA), BF16),
        scratch_shapes=[pltpu.VMEM((GLA_HEADS, GLA_DV, GLA_DK), F32)],
        compiler_params=_params("parallel", "arbitrary"),
        name="gla",
    )(proj, proj, g_lr, proj, proj, wg, bg, ng, sums, sel)


def _s5prep_kernel(lre_ref, lim_ref, ldt_ref, bre_ref, bim_ref, cre_ref, cim_ref,
                   are_ref, aim_ref, wb_ref, wc_ref):
    lre = lre_ref[...]
    lim = lim_ref[...]
    dt = jnp.exp(ldt_ref[...])
    z_re = lre * dt
    z_im = lim * dt
    mag = jnp.exp(z_re)
    ab_re = mag * jnp.cos(z_im)
    ab_im = mag * jnp.sin(z_im)
    den = lre * lre + lim * lim
    n_re = ab_re - 1.0
    n_im = ab_im
    f_re = (n_re * lre + n_im * lim) / den
    f_im = (n_im * lre - n_re * lim) / den
    b_re = bre_ref[...]
    b_im = bim_ref[...]
    are_ref[...] = ab_re
    aim_ref[...] = ab_im
    g, hh, p = b_re.shape
    rows2d = lambda t: t.reshape(g * hh, p).astype(BF16)
    bb = (rows2d(f_re * b_re - f_im * b_im), rows2d(f_re * b_im + f_im * b_re))
    cc = (rows2d(cre_ref[...]), rows2d(-cim_ref[...]))

    nsb = wb_ref.shape[0]
    blk_rows = S5_BLOCK_GROUPS * hh
    ri = lax.broadcasted_iota(jnp.int32, (p, S5_BLOCK_S), 0)
    ci = lax.broadcasted_iota(jnp.int32, (p, S5_BLOCK_S), 1)
    spread = jnp.where(ci % p == ri, 1.0, 0.0).astype(BF16)
    ri = lax.broadcasted_iota(jnp.int32, (S5_BLOCK_S, p), 0)
    ci = lax.broadcasted_iota(jnp.int32, (S5_BLOCK_S, p), 1)
    spread_t = jnp.where(ri % p == ci, 1.0, 0.0).astype(BF16)
    ri = lax.broadcasted_iota(jnp.int32, (blk_rows, S5_BLOCK_S), 0)
    ci = lax.broadcasted_iota(jnp.int32, (blk_rows, S5_BLOCK_S), 1)
    diag_b = (ri // hh) == (ci // p)
    ri = lax.broadcasted_iota(jnp.int32, (S5_BLOCK_S, blk_rows), 0)
    ci = lax.broadcasted_iota(jnp.int32, (S5_BLOCK_S, blk_rows), 1)
    diag_c = (ri // p) == (ci // hh)
    for sb in range(nsb):
        for hf in range(2):
            r0 = (hf * (g // 2) + sb * S5_BLOCK_GROUPS) * hh
            for part in range(2):
                y = jnp.where(diag_b, _dot(bb[part][r0:r0 + blk_rows], spread), 0.0)
                wb_ref[sb, hf * blk_rows:(hf + 1) * blk_rows,
                       part * S5_BLOCK_S:(part + 1) * S5_BLOCK_S] = y.astype(BF16)
                z = jnp.where(diag_c, _dot_nt(spread_t, cc[part][r0:r0 + blk_rows]), 0.0)
                wc_ref[sb, part * S5_BLOCK_S:(part + 1) * S5_BLOCK_S,
                       hf * blk_rows:(hf + 1) * blk_rows] = z.astype(BF16)


def _s5prep(lam_re, lam_im, log_dt, bt_re, bt_im, c_re, c_im):
    g, p = lam_re.shape
    nsb = g // (2 * S5_BLOCK_GROUPS)
    a_shape = jax.ShapeDtypeStruct((g, 1, p), F32)
    return pl.pallas_call(
        _s5prep_kernel,
        out_shape=(a_shape, a_shape,
                   jax.ShapeDtypeStruct((nsb, 2 * S5_BLOCK_U, 2 * S5_BLOCK_S), BF16),
                   jax.ShapeDtypeStruct((nsb, 2 * S5_BLOCK_S, 2 * S5_BLOCK_U), BF16)),
        compiler_params=pltpu.CompilerParams(vmem_limit_bytes=VMEM_LIMIT),
        name="s5prep",
    )(lam_re.reshape(g, 1, p), lam_im.reshape(g, 1, p), log_dt.reshape(g, 1, 1), bt_re, bt_im, c_re, c_im)


def _s5_kernel(u0_ref, u1_ref, wb_ref, wc_ref, a_ref, d_ref, o_ref, uf_ref, bu_ref, s_ref, y_ref, carry_ref, *, tc):
    @pl.when(pl.program_id(1) == 0)
    def _():
        carry_ref[...] = jnp.zeros_like(carry_ref)

    nb = u0_ref.shape[0]
    tile = SUBLANES * S5_SUB
    for b in range(nb):
        uf_ref[pl.ds(b, tc, stride=SUBLANES), :] = u0_ref[b].astype(F32)
        uf_ref[pl.ds(nb + b, tc, stride=SUBLANES), :] = u1_ref[b].astype(F32)

    lo = (lax.broadcasted_iota(jnp.int32, (tile, S5_BLOCK_U), 0) & nb) == 0
    a_re = a_ref[0, :, :S5_BLOCK_S]
    a_im = a_ref[0, :, S5_BLOCK_S:]
    s_re = carry_ref[:, :S5_BLOCK_S]
    s_im = carry_ref[:, S5_BLOCK_S:]

    for s in range(tc // S5_SUB):
        base = s * tile
        uf = uf_ref[base:base + tile, :]
        lhs = jnp.concatenate([jnp.where(lo, uf, 0.0), jnp.where(lo, 0.0, uf)], axis=1).astype(BF16)
        bu_ref[base:base + tile, :] = _dot(lhs, wb_ref[0])
        for p in range(S5_SUB // 2):
            r0 = base + p * 2 * SUBLANES
            r1 = r0 + SUBLANES
            m_re = a_re * s_re - a_im * s_im + bu_ref[r0:r1, :S5_BLOCK_S]
            m_im = a_re * s_im + a_im * s_re + bu_ref[r0:r1, S5_BLOCK_S:]
            s_re = a_re * m_re - a_im * m_im + bu_ref[r1:r1 + SUBLANES, :S5_BLOCK_S]
            s_im = a_re * m_im + a_im * m_re + bu_ref[r1:r1 + SUBLANES, S5_BLOCK_S:]
            s_ref[r0:r0 + 2 * SUBLANES, :S5_BLOCK_S] = jnp.concatenate([m_re, s_re], axis=0).astype(BF16)
            s_ref[r0:r0 + 2 * SUBLANES, S5_BLOCK_S:] = jnp.concatenate([m_im, s_im], axis=0).astype(BF16)
        y8 = _dot(s_ref[base:base + tile, :], wc_ref[0])
        y = jnp.where(lo, y8[:, :S5_BLOCK_U], y8[:, S5_BLOCK_U:])
        y = (y.reshape(S5_SUB, SUBLANES, S5_BLOCK_U)
             + d_ref[...] * uf.reshape(S5_SUB, SUBLANES, S5_BLOCK_U)).reshape(tile, S5_BLOCK_U)
        cdf = 0.5 * (1.0 + jnp.tanh(math.sqrt(2.0 / math.pi) * (y + 0.044715 * (y * y * y))))
        y_ref[base:base + tile, :] = y * cdf

    carry_ref[:, :S5_BLOCK_S] = s_re
    carry_ref[:, S5_BLOCK_S:] = s_im
    for j in range(2 * nb):
        o_ref[j] = y_ref[pl.ds(j, tc, stride=SUBLANES), :].astype(o_ref.dtype)


def _s5(proj3, wb, wc, a8, d8):
    bsz, seq, _ = proj3.shape
    nsb = wb.shape[0]
    tc = min(512, seq)
    rows = SUBLANES * tc
    cb = COL_U // S5_BLOCK_U
    return pl.pallas_call(
        functools.partial(_s5_kernel, tc=tc),
        grid=(nsb, seq // tc),
        in_specs=[pl.BlockSpec((bsz, tc, S5_BLOCK_U), lambda sb, i: (0, i, cb + sb)),
                  pl.BlockSpec((bsz, tc, S5_BLOCK_U), lambda sb, i: (0, i, cb + nsb + sb)),
                  pl.BlockSpec((1, 2 * S5_BLOCK_U, 2 * S5_BLOCK_S), lambda sb, i: (sb, 0, 0)),
                  pl.BlockSpec((1, 2 * S5_BLOCK_S, 2 * S5_BLOCK_U), lambda sb, i: (sb, 0, 0)),
                  pl.BlockSpec((1, SUBLANES, 2 * S5_BLOCK_S), lambda sb, i: (sb, 0, 0)),
                  pl.BlockSpec((1, SUBLANES, S5_BLOCK_U), lambda sb, i: (sb, 0, 0))],
        out_specs=pl.BlockSpec((2 * bsz, tc, S5_BLOCK_U), lambda sb, i: (0, i, sb)),
        out_shape=jax.ShapeDtypeStruct((2 * bsz, seq, nsb * S5_BLOCK_U), BF16),
        scratch_shapes=[pltpu.VMEM((rows, S5_BLOCK_U), F32),
                        pltpu.VMEM((rows, 2 * S5_BLOCK_S), F32),
                        pltpu.VMEM((rows, 2 * S5_BLOCK_S), BF16),
                        pltpu.VMEM((rows, S5_BLOCK_U), F32),
                        pltpu.VMEM((SUBLANES, 2 * S5_BLOCK_S), F32)],
        compiler_params=_params("parallel", "arbitrary"),
        name="s5",
    )(proj3, proj3, wb, wc, a8, d8)


def _glu_kernel(y0_ref, y1_ref, z_ref, w_ref, b_ref, o_ref):
    yb = jnp.concatenate([y0_ref[0], y1_ref[0]], axis=1)
    acc = _dot(yb, w_ref[...]) + b_ref[...]
    y = yb.astype(F32)
    z = z_ref[...].astype(F32)
    o_ref[...] = (y * z * (0.25 * (1.0 + jnp.tanh(0.5 * acc)) * (1.0 + jnp.tanh(0.5 * z)))).astype(o_ref.dtype)


def _glu(yv, proj, w, b, bsz, seq):
    d = w.shape[0]
    half = d // 2
    tm = min(1024, seq)
    nt = seq // tm
    return pl.pallas_call(
        _glu_kernel,
        grid=(bsz, nt),
        in_specs=[pl.BlockSpec((1, tm, half), lambda b_, i: (b_, i, 0)),
                  pl.BlockSpec((1, tm, half), lambda b_, i: (bsz + b_, i, 0)),
                  pl.BlockSpec((tm, d), lambda b_, i: (b_ * nt + i, COL_ZS // D_S5)),
                  pl.BlockSpec((d, d), lambda b_, i: (0, 0)),
                  pl.BlockSpec((1, d), lambda b_, i: (0, 0))],
        out_specs=pl.BlockSpec((tm, d), lambda b_, i: (b_ * nt + i, 0)),
        out_shape=jax.ShapeDtypeStruct((bsz * seq, d), BF16),
        compiler_params=_params("parallel", "parallel"),
        name="glu",
    )(yv, yv, proj, w, b)


OUT_EPILOGUE_CHUNKS = 2
OUT_TK = 1024


def _out_kernel(og_ref, os_ref, w_hbm, x_hbm, gate_ref, lg_ref, lb_ref, o_ref, w_buf, x_buf, w_sem, x_sem,
                *, tm, nt, n_tiles, nk):
    b = pl.program_id(0)
    i = pl.program_id(1)
    tile = b * nt + i
    half = nk // 2

    def w_copy(k, slot):
        return pltpu.make_async_copy(w_hbm.at[pl.ds(k * OUT_TK, OUT_TK), :], w_buf.at[slot], w_sem.at[slot])

    r0 = pl.multiple_of(i * tm, tm)
    x_copy = pltpu.make_async_copy(x_hbm.at[b, pl.ds(r0, tm), :], x_buf, x_sem)

    @pl.when(tile == 0)
    def _():
        w_copy(0, 0).start()

    x_copy.start()
    for k in range(nk):
        slot = k % 2
        w_copy(k, slot).wait()
        if k + 1 < nk:
            w_copy(k + 1, 1 - slot).start()
        else:
            @pl.when(tile + 1 < n_tiles)
            def _():
                w_copy(0, 1 - slot).start()
        src = og_ref if k < half else os_ref
        cols = slice((k % half) * OUT_TK, (k % half + 1) * OUT_TK)
        if k == 0:
            o_ref[0] = _dot(src[:, cols], w_buf[slot])
        elif k < nk - 1:
            o_ref[0] += _dot(src[:, cols], w_buf[slot])
        else:
            x_copy.wait()
            rows = tm // OUT_EPILOGUE_CHUNKS
            gate = gate_ref[0] * (1.0 / DEEPNORM_ALPHA)
            for c in range(OUT_EPILOGUE_CHUNKS):
                rs = slice(c * rows, (c + 1) * rows)
                mixed = o_ref[0, rs, :] + _dot(src[rs, cols], w_buf[slot])
                r = x_buf[rs, :] + gate * mixed
                mu = jnp.mean(r, axis=-1, keepdims=True)
                rc = r - mu
                var = jnp.mean(rc * rc, axis=-1, keepdims=True)
                o_ref[0, rs, :] = (rc * lax.rsqrt(var + NORM_EPS / DEEPNORM_ALPHA ** 2) * lg_ref[...]
                                   + lb_ref[...])


def _outproj(og, osb, w, x, mod3, lg, lb):
    bsz, seq, d = x.shape
    dh = og.shape[1]
    tm = min(512, seq)
    nt = seq // tm
    nk = 2 * dh // OUT_TK
    assert nk % 2 == 0 and nk >= 2
    return pl.pallas_call(
        functools.partial(_out_kernel, tm=tm, nt=nt, n_tiles=bsz * nt, nk=nk),
        grid=(bsz, nt),
        in_specs=[pl.BlockSpec((tm, dh), lambda b, i: (b * nt + i, 0)),
                  pl.BlockSpec((tm, dh), lambda b, i: (b * nt + i, 0)),
                  pl.BlockSpec(memory_space=pl.ANY),
                  pl.BlockSpec(memory_space=pl.ANY),
                  pl.BlockSpec((1, 1, d), lambda b, i: (b, 0, 2)),
                  pl.BlockSpec((1, d), lambda b, i: (0, 0)),
                  pl.BlockSpec((1, d), lambda b, i: (0, 0))],
        out_specs=pl.BlockSpec((1, tm, d), lambda b, i: (b, i, 0)),
        out_shape=jax.ShapeDtypeStruct(x.shape, x.dtype),
        scratch_shapes=[pltpu.VMEM((2, OUT_TK, d), BF16), pltpu.VMEM((tm, d), F32),
                        pltpu.SemaphoreType.DMA((2,)), pltpu.SemaphoreType.DMA(())],
        compiler_params=_params("arbitrary", "arbitrary"),
        name="outproj",
    )(og, osb, w, x, mod3, lg, lb)


def _s5_layouts(ab_re, ab_im, d_skip, bsz):
    g = ab_re.shape[0]
    nsb = g // (2 * S5_BLOCK_GROUPS)

    def rows8(v, width):
        t = v.reshape(2, nsb, 1, width).transpose(1, 0, 2, 3)
        return jnp.broadcast_to(t, (nsb, 2, bsz, width)).reshape(nsb, 2 * bsz, width)

    a8 = jnp.concatenate([rows8(ab_re.reshape(-1), S5_BLOCK_S), rows8(ab_im.reshape(-1), S5_BLOCK_S)], axis=2)
    d8 = rows8(d_skip, S5_BLOCK_U)
    return a8, d8


def kernel(x, c, w_ada, b_ada, w_in, w_gla_gate, b_gla_gate, gla_norm_g, s5_lambda_re, s5_lambda_im, s5_log_dt, s5_b_re, s5_b_im, s5_c_re, s5_c_im, s5_d, w_glu, b_glu, w_out, ln_g, ln_b):
    bsz, seq, d = x.shape
    assert bsz * 2 == SUBLANES and w_ada.shape[0] == DEPTH and s5_d.shape[1] == D_S5
    layer = 0

    mod = _ada(c.T, w_ada[layer], b_ada[layer][None, :])
    mod3 = mod.reshape(bsz, 1, 3 * d)

    w_main, w_glr = _wprep(jnp.swapaxes(w_in, 1, 2)[layer])
    proj, g_lr = _inproj(x, mod3, w_main, w_glr)

    wg = jnp.pad(w_gla_gate[layer], ((0, LANES - GLA_GATE_RANK), (0, 0))).astype(BF16)
    o_gla = _gla(proj, g_lr, wg, b_gla_gate[layer][None, :], gla_norm_g[layer][None, :], bsz, seq)

    bt_re = s5_b_re[layer].transpose(0, 2, 1)
    bt_im = s5_b_im[layer].transpose(0, 2, 1)
    ab_re, ab_im, wb, wc = _s5prep(s5_lambda_re[layer], s5_lambda_im[layer], s5_log_dt[layer],
                                   bt_re, bt_im, s5_c_re[layer], s5_c_im[layer])
    a8, d8 = _s5_layouts(ab_re, ab_im, s5_d[layer], bsz)
    yv = _s5(proj.reshape(bsz, seq, N_PROJ), wb, wc, a8, d8)
    o_s5 = _glu(yv, proj, w_glu[layer].astype(BF16), b_glu[layer][None, :], bsz, seq)

    return _outproj(o_gla, o_s5, w_out[layer].astype(BF16), x, mod3,
                    ln_g[layer][None, :], ln_b[layer][None, :])
```
